```python
import math
import jax, jax.numpy as jnp
from jax import lax
import numpy as np

D_MODEL = 1024
BATCH = 8
SEQ = 4096
DEPTH = 2

GRID_W = 64
CTX_LEN = 256

RET_HEADS = 4
RET_DK = 128
RET_DV = 256
RET_CHUNK = 128
RET_QK = RET_HEADS * RET_DK
RET_V = RET_HEADS * RET_DV

GLA_HEADS = 4
GLA_DK = 128
GLA_DV = 256
GLA_RANK = 16
GLA_CHUNK = 64
GLA_LOGIT_NORM = 16.0
GLA_QK = GLA_HEADS * GLA_DK
GLA_V = GLA_HEADS * GLA_DV

ROPE_BASE = 10000.0
EPS = 1e-6

IN_SPLITS = (RET_QK, RET_QK, RET_V, RET_V, GLA_QK, GLA_QK, GLA_V, GLA_V, GLA_RANK, D_MODEL, D_MODEL)
IN_WIDTH = 2 * RET_QK + 2 * RET_V + 2 * GLA_QK + 2 * GLA_V + GLA_RANK + 2 * D_MODEL

kernel_name = "hybrid_retention_gla_prefix_dit"


def rms_norm(x, gain):
    xf = x.astype(jnp.float32)
    y = xf * lax.rsqrt(jnp.mean(xf * xf, axis=-1, keepdims=True) + EPS)
    return (y * gain.astype(jnp.float32)).astype(x.dtype)


def modulation(cvec, w_ada, b_ada):
    m = jax.nn.silu(cvec) @ w_ada + b_ada
    return jnp.split(m, 3, axis=-1)


def split_columns(p):
    idx, acc = [], 0
    for w in IN_SPLITS[:-1]:
        acc += w
        idx.append(acc)
    return jnp.split(p, idx, axis=-1)


def to_heads(t, n_heads):
    b, l, w = t.shape
    return t.reshape(b, l, n_heads, w // n_heads).transpose(0, 2, 1, 3)


def from_heads(t):
    b, h, l, d = t.shape
    return t.transpose(0, 2, 1, 3).reshape(b, l, h * d)


def flip(t):
    return jnp.flip(t, axis=2)


def axial_rotary(rows):
    r_idx, c_idx = jnp.meshgrid(jnp.arange(rows), jnp.arange(GRID_W), indexing="ij")
    r_idx = r_idx.reshape(-1).astype(jnp.float32)
    c_idx = c_idx.reshape(-1).astype(jnp.float32)
    n_freq = RET_DK // 4
    inv_freq = ROPE_BASE ** (-jnp.arange(n_freq, dtype=jnp.float32) / n_freq)
    ang_r = r_idx[:, None] * inv_freq
    ang_c = c_idx[:, None] * inv_freq
    ang = jnp.stack([ang_r, ang_r, ang_c, ang_c], axis=1).reshape(-1, RET_DK)
    return jnp.cos(ang), jnp.sin(ang)


def apply_rotary(t, cos, sin):
    tr = t.reshape(*t.shape[:-1], 2, 2, RET_DK // 4)
    rot = jnp.concatenate([-tr[..., 1:, :], tr[..., :1, :]], axis=-2).reshape(t.shape)
    return t * cos + rot * sin


def retention_chunked(q, k, v, log_gamma, state0, strict):
    b, nh, seq, dk = q.shape
    dv = v.shape[-1]
    n = seq // RET_CHUNK
    qc = q.reshape(b, nh, n, RET_CHUNK, dk)
    kc = k.reshape(b, nh, n, RET_CHUNK, dk)
    vc = v.reshape(b, nh, n, RET_CHUNK, dv)
    pos = jnp.arange(RET_CHUNK, dtype=jnp.float32)
    diff = pos[:, None] - pos[None, :]
    keep = (diff > 0) if strict else (diff >= 0)
    decay = jnp.where(keep, jnp.exp(log_gamma[:, None, None] * jnp.where(keep, diff, 0.0)), 0.0)
    scores = jnp.einsum("bhnid,bhnjd->bhnij", qc, kc) * decay[None, :, None]
    intra = jnp.einsum("bhnij,bhnje->bhnie", scores, vc)
    lgv = log_gamma[None, :, None, None, None]
    k_dec = kc * jnp.exp(lgv * (RET_CHUNK - 1.0 - pos)[:, None])
    kv_chunk = jnp.einsum("bhncd,bhnce->bhnde", k_dec, vc)
    gamma_chunk = jnp.exp(log_gamma * RET_CHUNK)[None, :, None, None]

    def step(s, kv):
        return gamma_chunk * s + kv, s

    s_final, s_prev = lax.scan(step, state0, jnp.moveaxis(kv_chunk, 2, 0))
    s_prev = jnp.moveaxis(s_prev, 0, 2)
    q_dec = qc * jnp.exp(lgv * (pos + 1.0)[:, None])
    inter = jnp.einsum("bhncd,bhnde->bhnce", q_dec, s_prev)
    return (intra + inter).reshape(b, nh, seq, dv), s_final


def gla_chunked(q, k, v, log_a, state0, strict):
    b, nh, seq, dk = q.shape
    dv = v.shape[-1]
    n = seq // GLA_CHUNK
    qc = q.reshape(b, nh, n, GLA_CHUNK, dk)
    kc = k.reshape(b, nh, n, GLA_CHUNK, dk)
    vc = v.reshape(b, nh, n, GLA_CHUNK, dv)
    g = jnp.cumsum(log_a.reshape(b, nh, n, GLA_CHUNK, dk), axis=3)
    g_ref = g[:, :, :, GLA_CHUNK // 2 - 1:GLA_CHUNK // 2, :]
    g_last = g[:, :, :, -1:, :]
    q_rel = qc * jnp.exp(g - g_ref)
    k_rel = kc * jnp.exp(g_ref - g)
    pos = jnp.arange(GLA_CHUNK)
    keep = (pos[:, None] > pos[None, :]) if strict else (pos[:, None] >= pos[None, :])
    scores = jnp.where(keep, jnp.einsum("bhnid,bhnjd->bhnij", q_rel, k_rel), 0.0)
    intra = jnp.einsum("bhnij,bhnje->bhnie", scores, vc)
    kv_chunk = jnp.einsum("bhncd,bhnce->bhnde", kc * jnp.exp(g_last - g), vc)
    chunk_decay = jnp.exp(g_last[:, :, :, 0, :])

    def step(s, inp):
        a, kv = inp
        return a[..., None] * s + kv, s

    s_final, s_prev = lax.scan(step, state0, (jnp.moveaxis(chunk_decay, 2, 0), jnp.moveaxis(kv_chunk, 2, 0)))
    s_prev = jnp.moveaxis(s_prev, 0, 2)
    inter = jnp.einsum("bhncd,bhnde->bhnce", qc * jnp.exp(g), s_prev)
    return (intra + inter).reshape(b, nh, seq, dv), s_final


def gla_log_decay(z, w_up, b_up):
    logit = (z @ w_up + b_up).astype(jnp.float32)
    return to_heads(jax.nn.log_sigmoid(logit) / GLA_LOGIT_NORM, GLA_HEADS)


def token_mixers(u, w_in, ret_decay, gla_w_up, gla_b_up, rotary, init_states):
    p = u @ w_in
    rq, rk, rv, rg, gq, gk, gv, gg, glr, ma, mb = split_columns(p)
    s_rf0, s_rb0, s_gf0, s_gb0 = init_states
    rq = to_heads(rq, RET_HEADS) * RET_DK ** -0.5
    rk = to_heads(rk, RET_HEADS)
    if rotary is not None:
        cos, sin = rotary
        rq = apply_rotary(rq, cos, sin)
        rk = apply_rotary(rk, cos, sin)
    rv = to_heads(rv, RET_HEADS)
    log_gamma = jnp.log1p(-jnp.exp(ret_decay.astype(jnp.float32)))
    ret_f, s_rf = retention_chunked(rq, rk, rv, log_gamma[0], s_rf0, False)
    ret_b, s_rb = retention_chunked(flip(rq), flip(rk), flip(rv), log_gamma[1], s_rb0, True)
    ret = ret_f + flip(ret_b)
    gq = to_heads(gq, GLA_HEADS) * GLA_DK ** -0.5
    gk = to_heads(gk, GLA_HEADS)
    gv = to_heads(gv, GLA_HEADS)
    log_a_f = gla_log_decay(glr, gla_w_up[0], gla_b_up[0])
    log_a_b = gla_log_decay(glr, gla_w_up[1], gla_b_up[1])
    gla_f, s_gf = gla_chunked(gq, gk, gv, log_a_f, s_gf0, False)
    gla_b, s_gb = gla_chunked(flip(gq), flip(gk), flip(gv), flip(log_a_b), s_gb0, True)
    gla = gla_f + flip(gla_b)
    return (ret, gla, rg, gg, ma, mb), (s_rf, s_rb, s_gf, s_gb)


def head_group_norm(o, gain):
    of = o.astype(jnp.float32)
    mu = jnp.mean(of, axis=-1, keepdims=True)
    var = jnp.mean(jnp.square(of - mu), axis=-1, keepdims=True)
    return from_heads((of - mu) * lax.rsqrt(var + EPS)) * gain.astype(jnp.float32)


def head_rms_norm(o, gain):
    of = o.astype(jnp.float32)
    y = of * lax.rsqrt(jnp.mean(of * of, axis=-1, keepdims=True) + EPS)
    return from_heads(y) * gain.astype(jnp.float32)


def merge_branches(parts, ret_norm_gain, gla_norm_gain, w_branch_ret, w_branch_gla, w_out):
    ret, gla, rg, gg, ma, mb = parts
    dtype = rg.dtype
    y_ret = (head_group_norm(ret, ret_norm_gain).astype(dtype) * jax.nn.silu(rg)) @ w_branch_ret
    y_gla = (head_rms_norm(gla, gla_norm_gain).astype(dtype) * jax.nn.silu(gg)) @ w_branch_gla
    merged = jax.nn.sigmoid(ma) * y_ret + jax.nn.sigmoid(mb) * y_gla
    return merged @ w_out


def setup_inputs(seed: int = 0) -> dict:
    key = jax.random.key(seed)
    ks = jax.random.split(key, 18)

    def normal(k, shape, scale):
        return scale * jax.random.normal(k, shape, jnp.float32)

    ret_decay_base = -math.log(2.0) * (5.0 + jnp.arange(RET_HEADS, dtype=jnp.float32))
    return {
        "x": normal(ks[0], (BATCH, SEQ, D_MODEL), 1.0),
        "c": normal(ks[1], (BATCH, D_MODEL), 1.0),
        "ctx": normal(ks[2], (BATCH, CTX_LEN, D_MODEL), 1.0),
        "c_ctx": normal(ks[3], (D_MODEL,), 1.0),
        "norm_gain": 1.0 + normal(ks[4], (DEPTH, D_MODEL), 0.1),
        "w_ada": normal(ks[5], (DEPTH, D_MODEL, 3 * D_MODEL), 0.5 * D_MODEL ** -0.5),
        "b_ada": normal(ks[6], (DEPTH, 3 * D_MODEL), 0.02),
        "w_in": normal(ks[7], (DEPTH, D_MODEL, IN_WIDTH), D_MODEL ** -0.5),
        "ret_decay": ret_decay_base + normal(ks[8], (DEPTH, 2, RET_HEADS), 0.05),
        "gla_w_up": normal(ks[9], (DEPTH, 2, GLA_RANK, GLA_QK), GLA_RANK ** -0.5),
        "gla_b_up": normal(ks[10], (DEPTH, 2, GLA_QK), 0.1),
        "ret_norm_gain": 1.0 + normal(ks[11], (DEPTH, RET_V), 0.1),
        "gla_norm_gain": 1.0 + normal(ks[12], (DEPTH, GLA_V), 0.1),
        "w_branch_ret": normal(ks[13], (DEPTH, RET_V, D_MODEL), RET_V ** -0.5),
        "w_branch_gla": normal(ks[14], (DEPTH, GLA_V, D_MODEL), GLA_V ** -0.5),
        "w_out": normal(ks[15], (DEPTH, D_MODEL, D_MODEL), D_MODEL ** -0.5),
        "final_norm_gain": 1.0 + normal(ks[16], (D_MODEL,), 0.1),
    }


def reference(x, c, ctx, c_ctx, norm_gain, w_ada, b_ada, w_in, ret_decay, gla_w_up, gla_b_up,
              ret_norm_gain, gla_norm_gain, w_branch_ret, w_branch_gla, w_out, final_norm_gain):
    batch, n_latent = x.shape[0], x.shape[1]
    rows = n_latent // GRID_W
    rotary = axial_rotary(rows)
    zero_states = (
        jnp.zeros((batch, RET_HEADS, RET_DK, RET_DV), jnp.float32),
        jnp.zeros((batch, RET_HEADS, RET_DK, RET_DV), jnp.float32),
        jnp.zeros((batch, GLA_HEADS, GLA_DK, GLA_DV), jnp.float32),
        jnp.zeros((batch, GLA_HEADS, GLA_DK, GLA_DV), jnp.float32),
    )
    h_lat, h_ctx = x, ctx
    for l in range(DEPTH):
        shift_x, scale_x, gate_x = modulation(c, w_ada[l], b_ada[l])
        shift_c, scale_c, gate_c = modulation(c_ctx, w_ada[l], b_ada[l])
        u_ctx = rms_norm(h_ctx, norm_gain[l]) * (1.0 + scale_c) + shift_c
        ctx_parts, ctx_states = token_mixers(u_ctx, w_in[l], ret_decay[l], gla_w_up[l], gla_b_up[l],
                                             None, zero_states)
        u_lat = rms_norm(h_lat, norm_gain[l]) * (1.0 + scale_x[:, None, :]) + shift_x[:, None, :]
        lat_parts, _ = token_mixers(u_lat, w_in[l], ret_decay[l], gla_w_up[l], gla_b_up[l],
                                    rotary, ctx_states)
        h_lat = h_lat + gate_x[:, None, :] * merge_branches(
            lat_parts, ret_norm_gain[l], gla_norm_gain[l], w_branch_ret[l], w_branch_gla[l], w_out[l])
        if l < DEPTH - 1:
            h_ctx = h_ctx + gate_c * merge_branches(
                ctx_parts, ret_norm_gain[l], gla_norm_gain[l], w_branch_ret[l], w_branch_gla[l], w_out[l])
    return rms_norm(h_lat, final_norm_gain)
```

```python
import functools
import math

import jax
import jax.numpy as jnp
from jax import lax
from jax.experimental import pallas as pl
from jax.experimental.pallas import tpu as pltpu

D_MODEL = 1024
HEADS = 4
DK = 128
DV = 256
QK = HEADS * DK
VW = HEADS * DV
RANK = 16
RANK_PAD = 128
RET_CHUNK = 128
GLA_CHUNK = 64
LOGIT_NORM = 16.0
GRID_W = 64
ROPE_BASE = 10000.0
EPS = 1e-6

TOKENS_PER_STEP = 128
QKV_W = 2 * (2 * QK + VW)
HALF_W = QKV_W + RANK_PAD
VMEM_LIMIT_BYTES = 56 * 1024 * 1024

F32 = jnp.float32
BF16 = jnp.bfloat16


def _dot(a, b):
    return jnp.dot(a, b, preferred_element_type=F32)


def _dot_nt(a, b):
    return lax.dot_general(a, b, (((1,), (1,)), ((), ())), preferred_element_type=F32)


def _dot_tn(a, b):
    return lax.dot_general(a, b, (((0,), (0,)), ((), ())), preferred_element_type=F32)


def _mod_kernel(c_ref, w_ref, b_ref, o_ref):
    c = c_ref[...]
    s = c * jax.nn.sigmoid(c)
    o_ref[0] = jnp.dot(s, w_ref[0], preferred_element_type=F32,
                       precision=lax.Precision.HIGHEST) + b_ref[0]


def _modulation(cvecs, w_ada, b_ada):
    depth = w_ada.shape[0]
    n = cvecs.shape[0]
    return pl.pallas_call(
        _mod_kernel,
        grid=(depth, 3),
        in_specs=[
            pl.BlockSpec((n, D_MODEL), lambda l, j: (0, 0)),
            pl.BlockSpec((1, D_MODEL, D_MODEL), lambda l, j: (l, 0, j)),
            pl.BlockSpec((1, 1, D_MODEL), lambda l, j: (l, 0, j)),
        ],
        out_specs=pl.BlockSpec((1, n, D_MODEL), lambda l, j: (l, 0, j)),
        out_shape=jax.ShapeDtypeStruct((depth, n, 3 * D_MODEL), F32),
        name="adaln_modulation",
    )(cvecs, w_ada, b_ada.reshape(depth, 1, 3 * D_MODEL))


def _modulated_input(h, mod_ref, gain_ref):
    ms = jnp.mean(h * h, axis=-1, keepdims=True)
    y = h * lax.rsqrt(ms + EPS) * gain_ref[...]
    u = y * (1.0 + mod_ref[0, 1:2, :]) + mod_ref[0, 0:1, :]
    return u.astype(BF16)


def _log_sigmoid(x):
    return jnp.minimum(x, 0.0) - jnp.log1p(jnp.exp(-jnp.abs(x)))


def _mix_block(fwd, rq, rk, rv, gq, gk, gv, log_a, lg_row, state_ref, out_ref, row0):
    C = RET_CHUNK
    row = lax.broadcasted_iota(jnp.int32, (C, C), 0)
    col = lax.broadcasted_iota(jnp.int32, (C, C), 1)
    diff = (row - col) if fwd else (col - row)
    keep = (diff >= 0) if fwd else (diff > 0)
    dpos = jnp.where(keep, diff, 0).astype(F32)
    posf = lax.broadcasted_iota(jnp.int32, (C, QK), 0).astype(F32)
    if fwd:
        qexp, kexp = posf + 1.0, (C - 1.0) - posf
    else:
        qexp, kexp = C - posf, posf
    qdec = jnp.exp(lg_row * qexp)
    kdec = jnp.exp(lg_row * kexp)
    gam = jnp.exp(lg_row * float(C))

    for h in range(HEADS):
        sl = slice(h * DK, (h + 1) * DK)
        vs = slice(h * DV, (h + 1) * DV)
        q, k, v = rq[:, sl], rk[:, sl], rv[:, vs]
        decay = jnp.where(keep, jnp.exp(lg_row[:, sl] * dpos), 0.0)
        scores = _dot_nt(q.astype(BF16), k.astype(BF16)) * decay
        s_prev = state_ref[h]
        o = _dot(scores.astype(BF16), v) + _dot((q * qdec[:, sl]).astype(BF16), s_prev.astype(BF16))
        kv = _dot_tn((k * kdec[:, sl]).astype(BF16), v)
        g2 = jnp.concatenate([gam[:, sl], gam[:, sl]], axis=1)
        state_ref[h] = g2 * s_prev + kv
        out_ref[row0:row0 + C, vs] = o

    G = GLA_CHUNK
    r2 = lax.broadcasted_iota(jnp.int32, (C, 2 * C), 0)
    c2 = lax.broadcasted_iota(jnp.int32, (C, 2 * C), 1) & (C - 1)
    same = (r2 >> 6) == (c2 >> 6)
    tri = (c2 <= r2) if fwd else (c2 >= r2)
    csum = jnp.where(same & tri, 1.0, 0.0).astype(BF16)
    hi = log_a.astype(BF16)
    lo = (log_a - hi.astype(F32)).astype(BF16)
    g = _dot(csum, jnp.concatenate([hi, lo], axis=0))
    rg = lax.broadcasted_iota(jnp.int32, (G, G), 0)
    cg = lax.broadcasted_iota(jnp.int32, (G, G), 1)
    keep_g = (cg <= rg) if fwd else (cg > rg)
    ref_i, last_i = (G // 2 - 1, G - 1) if fwd else (G // 2, 0)
    subs = (0, 1) if fwd else (1, 0)
    for h in range(HEADS):
        sl = slice(h * DK, (h + 1) * DK)
        vs = slice(h * DV, (h + 1) * DV)
        for s in subs:
            rs = slice(s * G, (s + 1) * G)
            gs = g[rs, sl]
            g_ref = gs[ref_i:ref_i + 1, :]
            g_last = gs[last_i:last_i + 1, :]
            q, k, v = gq[rs, sl], gk[rs, sl], gv[rs, vs]
            q_rel = q * jnp.exp(gs - g_ref)
            k_rel = k * jnp.exp(g_ref - gs)
            k_st = k * jnp.exp(g_last - gs)
            q_in = q * jnp.exp(gs)
            scores = jnp.where(keep_g, _dot_nt(q_rel.astype(BF16), k_rel.astype(BF16)), 0.0)
            s_prev = state_ref[HEADS + h]
            o = _dot(scores.astype(BF16), v) + _dot(q_in.astype(BF16), s_prev.astype(BF16))
            kv = _dot_tn(k_st.astype(BF16), v)
            dcol = jnp.broadcast_to(jnp.exp(g_last), (DK, DK)).T
            state_ref[HEADS + h] = jnp.concatenate([dcol, dcol], axis=1) * s_prev + kv
            out_ref[row0 + s * G:row0 + (s + 1) * G, VW + h * DV:VW + (h + 1) * DV] = o


def _log_gamma_row(rd_ref):
    return jnp.log1p(-jnp.exp(rd_ref[...]))


def _gla_log_decay(glr, wup_ref, bup_ref):
    logit = _dot(glr.astype(BF16), wup_ref[...]) + bup_ref[...]
    return _log_sigmoid(logit) * (1.0 / LOGIT_NORM)


def _fwd_kernel(h_ref, mod_ref, gain_ref, w_ref, cos_ref, sina_ref, sinb_ref, rd_ref, wup_ref, bup_ref,
                s0_ref, qkv_ref, of_ref, sout_ref, state_ref):
    j = pl.program_id(1)

    @pl.when(j == 0)
    def _():
        state_ref[...] = s0_ref[0]

    u = _modulated_input(h_ref[0], mod_ref, gain_ref)
    cos, sina, sinb = cos_ref[...], sina_ref[...], sinb_ref[...]

    def rotary(t):
        parts = []
        for h in range(HEADS):
            th = t[:, h * DK:(h + 1) * DK]
            parts.append(th * cos + pltpu.roll(th, 96, 1) * sina + pltpu.roll(th, 32, 1) * sinb)
        return jnp.concatenate(parts, axis=1)

    rq = rotary(_dot(u, w_ref[:, 0:QK]) * (DK ** -0.5))
    rk = rotary(_dot(u, w_ref[:, QK:2 * QK]))
    rv = _dot(u, w_ref[:, 2 * QK:2 * QK + VW]).astype(BF16)
    o1 = 2 * QK + VW
    gq = _dot(u, w_ref[:, o1:o1 + QK]) * (DK ** -0.5)
    gk = _dot(u, w_ref[:, o1 + QK:o1 + 2 * QK])
    gv = _dot(u, w_ref[:, o1 + 2 * QK:QKV_W]).astype(BF16)
    glr = _dot(u, w_ref[:, QKV_W:HALF_W])

    qkv_ref[0, :, 0:QK] = rq.astype(BF16)
    qkv_ref[0, :, QK:2 * QK] = rk.astype(BF16)
    qkv_ref[0, :, 2 * QK:o1] = rv
    qkv_ref[0, :, o1:o1 + QK] = gq.astype(BF16)
    qkv_ref[0, :, o1 + QK:o1 + 2 * QK] = gk.astype(BF16)
    qkv_ref[0, :, o1 + 2 * QK:QKV_W] = gv

    log_a = _gla_log_decay(glr, wup_ref, bup_ref)
    _mix_block(True, rq, rk, rv, gq, gk, gv, log_a, _log_gamma_row(rd_ref), state_ref, of_ref.at[0], 0)

    @pl.when(j == pl.num_programs(1) - 1)
    def _():
        sout_ref[0] = state_ref[...]


def _forward_sweep(h, mod, gain, w1, rot, rd_row, wup, bup, s0):
    b, n_tok, _ = h.shape
    t = TOKENS_PER_STEP
    nc = n_tok // t
    const2 = lambda bi, j: (0, 0)
    return pl.pallas_call(
        _fwd_kernel,
        grid=(b, nc),
        in_specs=[
            pl.BlockSpec((1, t, D_MODEL), lambda bi, j: (bi, j, 0)),
            pl.BlockSpec((1, 3, D_MODEL), lambda bi, j: (bi, 0, 0)),
            pl.BlockSpec((1, D_MODEL), const2),
            pl.BlockSpec((D_MODEL, HALF_W), const2),
            pl.BlockSpec((t, DK), lambda bi, j: (j, 0)),
            pl.BlockSpec((t, DK), lambda bi, j: (j, 0)),
            pl.BlockSpec((t, DK), lambda bi, j: (j, 0)),
            pl.BlockSpec((1, QK), const2),
            pl.BlockSpec((RANK_PAD, QK), const2),
            pl.BlockSpec((1, QK), const2),
            pl.BlockSpec((1, 2 * HEADS, DK, DV), lambda bi, j: (bi, 0, 0, 0)),
        ],
        out_specs=[
            pl.BlockSpec((1, t, QKV_W), lambda bi, j: (bi, j, 0)),
            pl.BlockSpec((1, t, 2 * VW), lambda bi, j: (bi, j, 0)),
            pl.BlockSpec((1, 2 * HEADS, DK, DV), lambda bi, j: (bi, 0, 0, 0)),
        ],
        out_shape=[
            jax.ShapeDtypeStruct((b, n_tok, QKV_W), BF16),
            jax.ShapeDtypeStruct((b, n_tok, 2 * VW), F32),
            jax.ShapeDtypeStruct((b, 2 * HEADS, DK, DV), F32),
        ],
        scratch_shapes=[pltpu.VMEM((2 * HEADS, DK, DV), F32)],
        compiler_params=pltpu.CompilerParams(
            dimension_semantics=("arbitrary", "arbitrary"), vmem_limit_bytes=VMEM_LIMIT_BYTES),
        name="forward_sweep",
    )(h, mod, gain, w1, *rot, rd_row, wup, bup, s0)


def _bwd_kernel(h_ref, mod_ref, gain_ref, w_ref, qkv_ref, of_ref, rd_ref, wup_ref, bup_ref,
                rng_ref, gng_ref, wbr_ref, wbg_ref, wo_ref, fng_ref, s0_ref,
                hout_ref, sout_ref, state_ref, ob_ref, *, need_out, final_norm):
    j = pl.program_id(1)

    @pl.when(j == 0)
    def _():
        state_ref[...] = s0_ref[0]

    h = h_ref[0]
    u = _modulated_input(h, mod_ref, gain_ref)
    glr = _dot(u, w_ref[:, QKV_W:HALF_W])
    log_a = _gla_log_decay(glr, wup_ref, bup_ref)

    o1 = 2 * QK + VW
    rq = qkv_ref[0, :, 0:QK].astype(F32)
    rk = qkv_ref[0, :, QK:2 * QK].astype(F32)
    rv = qkv_ref[0, :, 2 * QK:o1]
    gq = qkv_ref[0, :, o1:o1 + QK].astype(F32)
    gk = qkv_ref[0, :, o1 + QK:o1 + 2 * QK].astype(F32)
    gv = qkv_ref[0, :, o1 + 2 * QK:QKV_W]
    _mix_block(False, rq, rk, rv, gq, gk, gv, log_a, _log_gamma_row(rd_ref), state_ref, ob_ref, 0)

    @pl.when(j == pl.num_programs(1) - 1)
    def _():
        sout_ref[0] = state_ref[...]

    if not need_out:
        hout_ref[0] = h
        return

    def head_norm(x, gain_row, center):
        parts = []
        for hh in range(HEADS):
            xh = x[:, hh * DV:(hh + 1) * DV]
            if center:
                xh = xh - jnp.mean(xh, axis=-1, keepdims=True)
            var = jnp.mean(xh * xh, axis=-1, keepdims=True)
            parts.append(xh * lax.rsqrt(var + EPS))
        return jnp.concatenate(parts, axis=1) * gain_row

    def silu(x):
        return x * jax.nn.sigmoid(x)

    ret = of_ref[0, :, 0:VW] + ob_ref[:, 0:VW]
    gla = of_ref[0, :, VW:2 * VW] + ob_ref[:, VW:2 * VW]
    gate_r = silu(_dot(u, w_ref[:, 0:VW]))
    y_ret = _dot((head_norm(ret, rng_ref[...], True) * gate_r).astype(BF16), wbr_ref[...])
    gate_g = silu(_dot(u, w_ref[:, VW:2 * VW]))
    y_gla = _dot((head_norm(gla, gng_ref[...], False) * gate_g).astype(BF16), wbg_ref[...])
    ma = _dot(u, w_ref[:, 2 * VW:2 * VW + D_MODEL])
    mb = _dot(u, w_ref[:, 2 * VW + D_MODEL:QKV_W])
    merged = jax.nn.sigmoid(ma) * y_ret + jax.nn.sigmoid(mb) * y_gla
    h_new = h + mod_ref[0, 2:3, :] * _dot(merged.astype(BF16), wo_ref[...])
    if final_norm:
        ms = jnp.mean(h_new * h_new, axis=-1, keepdims=True)
        h_new = h_new * lax.rsqrt(ms + EPS) * fng_ref[...]
    hout_ref[0] = h_new


def _backward_sweep(h, mod, gain, w2, qkv, o_fwd, rd_row, wup, bup, ret_gain, gla_gain,
                    w_br, w_bg, w_o, final_gain, s0, *, need_out, final_norm):
    b, n_tok, _ = h.shape
    t = TOKENS_PER_STEP
    nc = n_tok // t
    const2 = lambda bi, j: (0, 0)
    rev = lambda bi, j: (bi, nc - 1 - j, 0)
    return pl.pallas_call(
        functools.partial(_bwd_kernel, need_out=need_out, final_norm=final_norm),
        grid=(b, nc),
        in_specs=[
            pl.BlockSpec((1, t, D_MODEL), rev),
            pl.BlockSpec((1, 3, D_MODEL), lambda bi, j: (bi, 0, 0)),
            pl.BlockSpec((1, D_MODEL), const2),
            pl.BlockSpec((D_MODEL, HALF_W), const2),
            pl.BlockSpec((1, t, QKV_W), rev),
            pl.BlockSpec((1, t, 2 * VW), rev),
            pl.BlockSpec((1, QK), const2),
            pl.BlockSpec((RANK_PAD, QK), const2),
            pl.BlockSpec((1, QK), const2),
            pl.BlockSpec((1, VW), const2),
            pl.BlockSpec((1, VW), const2),
            pl.BlockSpec((VW, D_MODEL), const2),
            pl.BlockSpec((VW, D_MODEL), const2),
            pl.BlockSpec((D_MODEL, D_MODEL), const2),
            pl.BlockSpec((1, D_MODEL), const2),
            pl.BlockSpec((1, 2 * HEADS, DK, DV), lambda bi, j: (bi, 0, 0, 0)),
        ],
        out_specs=[
            pl.BlockSpec((1, t, D_MODEL), rev),
            pl.BlockSpec((1, 2 * HEADS, DK, DV), lambda bi, j: (bi, 0, 0, 0)),
        ],
        out_shape=[
            jax.ShapeDtypeStruct((b, n_tok, D_MODEL), F32),
            jax.ShapeDtypeStruct((b, 2 * HEADS, DK, DV), F32),
        ],
        scratch_shapes=[pltpu.VMEM((2 * HEADS, DK, DV), F32), pltpu.VMEM((t, 2 * VW), F32)],
        compiler_params=pltpu.CompilerParams(
            dimension_semantics=("arbitrary", "arbitrary"), vmem_limit_bytes=VMEM_LIMIT_BYTES),
        name="backward_sweep",
    )(h, mod, gain, w2, qkv, o_fwd, rd_row, wup, bup, ret_gain, gla_gain, w_br, w_bg, w_o, final_gain, s0)


def _rotary_tables(n_rows):
    r_idx, c_idx = jnp.meshgrid(jnp.arange(n_rows), jnp.arange(GRID_W), indexing="ij")
    r_idx = r_idx.reshape(-1).astype(F32)
    c_idx = c_idx.reshape(-1).astype(F32)
    n_freq = DK // 4
    inv_freq = ROPE_BASE ** (-jnp.arange(n_freq, dtype=F32) / n_freq)
    ang_r = r_idx[:, None] * inv_freq
    ang_c = c_idx[:, None] * inv_freq
    ang = jnp.stack([ang_r, ang_r, ang_c, ang_c], axis=1).reshape(-1, DK)
    cos, sin = jnp.cos(ang), jnp.sin(ang)
    first = (jnp.arange(DK) // n_freq) % 2 == 0
    return cos, jnp.where(first, -sin, 0.0), jnp.where(first, 0.0, sin)


def _split_w_in(w):
    idx = [0]
    for width in (QK, QK, VW, VW, QK, QK, VW, VW, RANK, D_MODEL, D_MODEL):
        idx.append(idx[-1] + width)
    blk = [w[:, idx[i]:idx[i + 1]] for i in range(11)]
    rq, rk, rv, rg, gq, gk, gv, gg, glr, ma, mb = blk
    glr = jnp.pad(glr, ((0, 0), (0, RANK_PAD - RANK)))
    w1 = jnp.concatenate([rq, rk, rv, gq, gk, gv, glr], axis=1).astype(BF16)
    w2 = jnp.concatenate([rg, gg, ma, mb, glr], axis=1).astype(BF16)
    return w1, w2


def kernel(x, c, ctx, c_ctx, norm_gain, w_ada, b_ada, w_in, ret_decay, gla_w_up, gla_b_up, ret_norm_gain,
           gla_norm_gain, w_branch_ret, w_branch_gla, w_out, final_norm_gain):
    batch, n_lat, _ = x.shape
    n_ctx = ctx.shape[1]
    depth = w_in.shape[0]

    cvecs = jnp.concatenate([c, c_ctx[None, :], jnp.zeros((16 - batch - 1, D_MODEL), F32)], axis=0)
    mods = _modulation(cvecs, w_ada, b_ada)

    rot_lat = _rotary_tables(n_lat // GRID_W)
    rot_ctx = (jnp.ones((n_ctx, DK), F32), jnp.zeros((n_ctx, DK), F32), jnp.zeros((n_ctx, DK), F32))
    zero_state = jnp.zeros((batch, 2 * HEADS, DK, DV), F32)
    fgain = final_norm_gain.reshape(1, D_MODEL)

    h_lat, h_ctx = x, ctx
    for l in range(depth):
        last = l == depth - 1
        mod_lat = mods[l, :batch].reshape(batch, 3, D_MODEL)
        mod_ctx = jnp.broadcast_to(mods[l, batch].reshape(1, 3, D_MODEL), (batch, 3, D_MODEL))
        gain = norm_gain[l].reshape(1, D_MODEL)
        w1, w2 = _split_w_in(w_in[l])
        rd_f = jnp.repeat(ret_decay[l, 0], DK).reshape(1, QK)
        rd_b = jnp.repeat(ret_decay[l, 1], DK).reshape(1, QK)
        wup = jnp.pad(gla_w_up[l], ((0, 0), (0, RANK_PAD - RANK), (0, 0))).astype(BF16)
        bup = gla_b_up[l].reshape(2, 1, QK)
        tail = (rd_b, wup[1], bup[1], ret_norm_gain[l].reshape(1, VW), gla_norm_gain[l].reshape(1, VW),
                w_branch_ret[l].astype(BF16), w_branch_gla[l].astype(BF16), w_out[l].astype(BF16), fgain)

        qkv_c, of_c, sf_c = _forward_sweep(h_ctx, mod_ctx, gain, w1, rot_ctx, rd_f, wup[0], bup[0], zero_state)
        qkv_l, of_l, _ = _forward_sweep(h_lat, mod_lat, gain, w1, rot_lat, rd_f, wup[0], bup[0], sf_c)
        h_ctx_new, sb_c = _backward_sweep(h_ctx, mod_ctx, gain, w2, qkv_c, of_c, *tail, zero_state,
                                          need_out=not last, final_norm=False)
        h_lat, _ = _backward_sweep(h_lat, mod_lat, gain, w2, qkv_l, of_l, *tail, sb_c,
                                   need_out=True, final_norm=last)
        h_ctx = h_ctx_new
    return h_lat
```

```python
import functools

import jax
import jax.numpy as jnp
from jax import lax
from jax.experimental import pallas as pl
from jax.experimental.pallas import tpu as pltpu

D_MODEL = 1024
HEADS = 4
DK = 128
DV = 256
QK = HEADS * DK
VW = HEADS * DV
RANK = 16
RANK_PAD = 128
CHUNK = 128
GLA_CHUNK = 64
LOGIT_NORM = 16.0
GRID_W = 64
ROPE_BASE = 10000.0
EPS = 1e-6

QKV_W = 2 * (2 * QK + VW)
HALF_W = QKV_W + RANK_PAD
SLICE_W = 512
V_OFF = 2 * QK
G_OFF = 2 * QK + VW
VMEM_LIMIT_BYTES = 56 * 1024 * 1024

F32 = jnp.float32
BF16 = jnp.bfloat16


def _dot(a, b):
    return jnp.dot(a, b, preferred_element_type=F32)


def _dot_nt(a, b):
    return lax.dot_general(a, b, (((1,), (1,)), ((), ())), preferred_element_type=F32)


def _dot_tn(a, b):
    return lax.dot_general(a, b, (((0,), (0,)), ((), ())), preferred_element_type=F32)


def _mod_kernel(c_ref, w_ref, b_ref, o_ref):
    c = c_ref[...]
    s = c * jax.nn.sigmoid(c)
    o_ref[0] = jnp.dot(s, w_ref[0], preferred_element_type=F32,
                       precision=lax.Precision.HIGHEST) + b_ref[0]


def _modulation(cvecs, w_ada, b_ada):
    depth = w_ada.shape[0]
    n = cvecs.shape[0]
    return pl.pallas_call(
        _mod_kernel,
        grid=(depth, 3),
        in_specs=[
            pl.BlockSpec((n, D_MODEL), lambda l, j: (0, 0)),
            pl.BlockSpec((1, D_MODEL, D_MODEL), lambda l, j: (l, 0, j)),
            pl.BlockSpec((1, 1, D_MODEL), lambda l, j: (l, 0, j)),
        ],
        out_specs=pl.BlockSpec((1, n, D_MODEL), lambda l, j: (l, 0, j)),
        out_shape=jax.ShapeDtypeStruct((depth, n, 3 * D_MODEL), F32),
        name="adaln_modulation",
    )(cvecs, w_ada, b_ada.reshape(depth, 1, 3 * D_MODEL))


def _modulated_input(h, mod_ref, gain_ref):
    ms = jnp.mean(h * h, axis=-1, keepdims=True)
    y = h * lax.rsqrt(ms + EPS) * gain_ref[...]
    u = y * (1.0 + mod_ref[0, 1:2, :]) + mod_ref[0, 0:1, :]
    return u.astype(BF16)


def _log_sigmoid(x):
    return jnp.minimum(x, 0.0) - jnp.log1p(jnp.exp(-jnp.abs(x)))


def _gla_decay_units(fwd, u, w_ref, wup_ref, bup_ref, store):
    hold = {}

    def low_rank():
        hold["glr"] = _dot(u, w_ref[:, QKV_W:HALF_W]).astype(BF16)

    def logit():
        x = _dot(hold.pop("glr"), wup_ref[...]) + bup_ref[...]
        log_a = _log_sigmoid(x) * (1.0 / LOGIT_NORM)
        hi = log_a.astype(BF16)
        lo = (log_a - hi.astype(F32)).astype(BF16)
        hold["hilo"] = jnp.concatenate([hi, lo], axis=0)

    def cumsum():
        r = lax.broadcasted_iota(jnp.int32, (CHUNK, 2 * CHUNK), 0)
        c = lax.broadcasted_iota(jnp.int32, (CHUNK, 2 * CHUNK), 1) & (CHUNK - 1)
        same = (r >> 6) == (c >> 6)
        tri = (c <= r) if fwd else (c >= r)
        ones = jnp.where(same & tri, 1.0, 0.0).astype(BF16)
        store(_dot(ones, hold.pop("hilo")))

    return low_rank, logit, cumsum


def _scan_units(fwd, rq, rk, rv, gq, gk, gv, g, lg_row, state_ref, sink):
    C = CHUNK
    row = lax.broadcasted_iota(jnp.int32, (C, C), 0)
    col = lax.broadcasted_iota(jnp.int32, (C, C), 1)
    diff = (row - col) if fwd else (col - row)
    keep = (diff >= 0) if fwd else (diff > 0)
    dpos = jnp.where(keep, diff, 0).astype(F32)
    posf = lax.broadcasted_iota(jnp.int32, (C, QK), 0).astype(F32)
    if fwd:
        qexp, kexp = posf + 1.0, (C - 1.0) - posf
    else:
        qexp, kexp = C - posf, posf
    q_dec = (rq * jnp.exp(lg_row * qexp)).astype(BF16)
    k_dec = (rk * jnp.exp(lg_row * kexp)).astype(BF16)
    gam = jnp.exp(lg_row * float(C))
    rq_b, rk_b = rq.astype(BF16), rk.astype(BF16)

    G = GLA_CHUNK
    ref_i, last_i = (G // 2 - 1, G - 1) if fwd else (G // 2, 0)

    def rows_of(i):
        top = jnp.broadcast_to(g[i:i + 1, :], (G, QK))
        bot = jnp.broadcast_to(g[G + i:G + i + 1, :], (G, QK))
        return jnp.concatenate([top, bot], axis=0)

    g_ref_rows = rows_of(ref_i)
    g_last_rows = rows_of(last_i)
    last_top = g[last_i:last_i + 1, :]
    last_bot = g[G + last_i:G + last_i + 1, :]
    last_p, last_q = (last_top, last_bot) if fwd else (last_bot, last_top)
    is_top = lax.broadcasted_iota(jnp.int32, (C, QK), 0) < G
    in_p = is_top if fwd else jnp.logical_not(is_top)
    q_rel = (gq * jnp.exp(g - g_ref_rows)).astype(BF16)
    k_rel = (gk * jnp.exp(g_ref_rows - g)).astype(BF16)
    q_loc = gq * jnp.exp(g)
    k_st = gk * jnp.exp(g_last_rows - g)
    q_in = (q_loc * jnp.where(in_p, 1.0, jnp.exp(last_p))).astype(BF16)
    k_upd = (k_st * jnp.where(in_p, jnp.exp(last_q), 1.0)).astype(BF16)
    q_loc, k_st = q_loc.astype(BF16), k_st.astype(BF16)
    d_tot = jnp.exp(last_p + last_q)
    same = (row >> 6) == (col >> 6)
    keep_g = same & ((col <= row) if fwd else (col > row))
    cross = ((row >= G) & (col < G)) if fwd else ((row < G) & (col >= G))
    q_rows = slice(G, C) if fwd else slice(0, G)

    units, hold = {}, {}

    def add(h):
        sl = slice(h * DK, (h + 1) * DK)
        vs = slice(h * DV, (h + 1) * DV)

        def ret_s():
            decay = jnp.where(keep, jnp.exp(lg_row[:, sl] * dpos), 0.0)
            hold["rs", h] = (_dot_nt(rq_b[:, sl], rk_b[:, sl]) * decay).astype(BF16)

        def ret_kv():
            hold["rkv", h] = _dot_tn(k_dec[:, sl], rv[:, vs])

        def ret_o():
            s_prev = state_ref[h]
            lhs = jnp.concatenate([hold.pop(("rs", h)), q_dec[:, sl]], axis=1)
            o = _dot(lhs, jnp.concatenate([rv[:, vs], s_prev.astype(BF16)], axis=0))
            state_ref[h] = jnp.concatenate([gam[:, sl], gam[:, sl]], axis=1) * s_prev + hold.pop(("rkv", h))
            sink("ret", h, o)

        def gla_s():
            local = _dot_nt(q_rel[:, sl], k_rel[:, sl])
            xq = _dot_nt(q_loc[q_rows, sl], k_st[:, sl])
            zeros_half = jnp.zeros((G, C), F32)
            xfull = jnp.concatenate([zeros_half, xq] if fwd else [xq, zeros_half], axis=0)
            hold["gs", h] = jnp.where(keep_g, local, jnp.where(cross, xfull, 0.0)).astype(BF16)

        def gla_kv():
            hold["gkv", h] = _dot_tn(k_upd[:, sl], gv[:, vs])

        def gla_o():
            s_prev = state_ref[HEADS + h]
            lhs = jnp.concatenate([hold.pop(("gs", h)), q_in[:, sl]], axis=1)
            o = _dot(lhs, jnp.concatenate([gv[:, vs], s_prev.astype(BF16)], axis=0))
            dcol = jnp.broadcast_to(d_tot[:, sl], (DK, DK)).T
            state_ref[HEADS + h] = jnp.concatenate([dcol, dcol], axis=1) * s_prev + hold.pop(("gkv", h))
            sink("gla", h, o)

        units.update({("ret", "s", h): ret_s, ("ret", "kv", h): ret_kv, ("ret", "o", h): ret_o,
                      ("gla", "s", h): gla_s, ("gla", "kv", h): gla_kv, ("gla", "o", h): gla_o})

    for h in range(HEADS):
        add(h)
    return units


def _log_gamma_row(rd_ref):
    return jnp.log1p(-jnp.exp(rd_ref[...]))


def _pipelined(i, step):
    parity = lax.rem(i, 2)

    @pl.when(parity == 0)
    def _():
        step(0, 1)

    @pl.when(parity == 1)
    def _():
        step(1, 0)


def _scan_order(fill):
    pairs = [(b, h) for b in ("ret", "gla") for h in range(HEADS)]
    fill = list(fill)
    order = [pairs[0] + ("s",), pairs[0] + ("kv",)]
    for p in range(len(pairs)):
        if p + 1 < len(pairs):
            order += [pairs[p + 1] + ("s",), pairs[p + 1] + ("kv",)]
        order.append(pairs[p] + ("o",))
        if fill:
            order.append(fill.pop(0))
    return order + fill


def _grouped(groups):
    return [(lambda fs=tuple(fs): [f() for f in fs]) for fs in groups]


def _run(order, units):
    for name in order:
        if callable(name):
            name()
        else:
            b, kind, h = name[0], name[2], name[1]
            units[b, kind, h]()


def _fwd_kernel(h_ref, mod_ref, gain_ref, w_ref, cos_ref, sina_ref, sinb_ref, rd_ref, wup_ref, bup_ref,
                s0_ref, qkv_ref, of_ref, sout_ref, state_ref, stage_ref, g_ref, *, nc):
    i = pl.program_id(0)
    j2 = lax.rem(jnp.maximum(i - 1, 0), nc)

    @pl.when(i == 0)
    def _():
        stage_ref[1] = jnp.zeros(stage_ref.shape[1:], BF16)
        g_ref[1] = jnp.zeros(g_ref.shape[1:], F32)

    @pl.when(j2 == 0)
    def _():
        state_ref[...] = s0_ref[0]

    def step(wr, rd):
        u = _modulated_input(h_ref[0], mod_ref, gain_ref)
        cos, sina, sinb = cos_ref[...], sina_ref[...], sinb_ref[...]

        def rotary(t):
            parts = []
            for h in range(HEADS):
                th = t[:, h * DK:(h + 1) * DK]
                parts.append(th * cos + pltpu.roll(th, 96, 1) * sina + pltpu.roll(th, 32, 1) * sinb)
            return jnp.concatenate(parts, axis=1)

        def proj(lo, post):
            def run():
                val = post(_dot(u, w_ref[:, lo:lo + SLICE_W])).astype(BF16)
                qkv_ref[0, :, lo:lo + SLICE_W] = val
                stage_ref[wr, :, lo:lo + SLICE_W] = val
            return run

        scale = DK ** -0.5
        post = {0: lambda t: rotary(t * scale), QK: rotary, G_OFF: lambda t: t * scale}
        slices = [proj(lo, post.get(lo, lambda t: t)) for lo in range(0, QKV_W, SLICE_W)]

        def store_g(val):
            g_ref[wr] = val

        low_rank, logit, cumsum = _gla_decay_units(True, u, w_ref, wup_ref, bup_ref, store_g)

        def sink(branch, h, o):
            off = (0 if branch == "ret" else VW) + h * DV
            of_ref[0, :, off:off + DV] = o

        units = _scan_units(
            True,
            stage_ref[rd, :, 0:QK].astype(F32), stage_ref[rd, :, QK:V_OFF].astype(F32),
            stage_ref[rd, :, V_OFF:G_OFF],
            stage_ref[rd, :, G_OFF:G_OFF + QK].astype(F32), stage_ref[rd, :, G_OFF + QK:G_OFF + 2 * QK].astype(F32),
            stage_ref[rd, :, G_OFF + 2 * QK:QKV_W],
            g_ref[rd], _log_gamma_row(rd_ref), state_ref, sink)

        fill = [[low_rank, slices[0]], [logit, slices[1]], [cumsum, slices[2]]] + [[s] for s in slices[3:]]
        _run(_scan_order(_grouped(fill)), units)

    _pipelined(i, step)

    @pl.when(j2 == nc - 1)
    def _():
        sout_ref[0] = state_ref[...]


def _const_spec(shape):
    return pl.BlockSpec(shape, lambda i: (0,) * len(shape), pipeline_mode=pl.Buffered(1))


def _forward_sweep(h, mod, gain, w1, rot, rd_row, wup, bup, s0):
    b, n_tok, _ = h.shape
    nc = n_tok // CHUNK
    n = b * nc
    c1 = lambda i: jnp.minimum(i, n - 1)
    c2 = lambda i: jnp.maximum(i - 1, 0)
    return pl.pallas_call(
        functools.partial(_fwd_kernel, nc=nc),
        grid=(n + 1,),
        in_specs=[
            pl.BlockSpec((1, CHUNK, D_MODEL), lambda i: (c1(i), 0, 0)),
            pl.BlockSpec((1, 3, D_MODEL), lambda i: (c1(i) // nc, 0, 0)),
            _const_spec((1, D_MODEL)),
            _const_spec((D_MODEL, HALF_W)),
            pl.BlockSpec((CHUNK, DK), lambda i: (c1(i) % nc, 0)),
            pl.BlockSpec((CHUNK, DK), lambda i: (c1(i) % nc, 0)),
            pl.BlockSpec((CHUNK, DK), lambda i: (c1(i) % nc, 0)),
            _const_spec((1, QK)),
            _const_spec((RANK_PAD, QK)),
            _const_spec((1, QK)),
            pl.BlockSpec((1, 2 * HEADS, DK, DV), lambda i: (c2(i) // nc, 0, 0, 0)),
        ],
        out_specs=[
            pl.BlockSpec((1, CHUNK, QKV_W), lambda i: (c1(i), 0, 0)),
            pl.BlockSpec((1, CHUNK, 2 * VW), lambda i: (c2(i), 0, 0)),
            pl.BlockSpec((1, 2 * HEADS, DK, DV), lambda i: (c2(i) // nc, 0, 0, 0)),
        ],
        out_shape=[
            jax.ShapeDtypeStruct((n, CHUNK, QKV_W), BF16),
            jax.ShapeDtypeStruct((n, CHUNK, 2 * VW), F32),
            jax.ShapeDtypeStruct((b, 2 * HEADS, DK, DV), F32),
        ],
        scratch_shapes=[
            pltpu.VMEM((2 * HEADS, DK, DV), F32),
            pltpu.VMEM((2, CHUNK, QKV_W), BF16),
            pltpu.VMEM((2, CHUNK, QK), F32),
        ],
        compiler_params=pltpu.CompilerParams(
            dimension_semantics=("arbitrary",), vmem_limit_bytes=VMEM_LIMIT_BYTES),
        name="forward_sweep",
    )(h.reshape(n, CHUNK, D_MODEL), mod, gain, w1, *rot, rd_row, wup, bup, s0)


def _bwd_kernel(h1_ref, h2_ref, mod1_ref, mod2_ref, gain_ref, w_ref, qkv_ref, of_ref, rd_ref, wup_ref, bup_ref,
                rng_ref, gng_ref, wbr_ref, wbg_ref, wo_ref, fng_ref, s0_ref,
                *refs, nc, need_out, final_norm):
    if need_out:
        hout_ref, sout_ref, state_ref, g_ref, gate_ref = refs
    else:
        sout_ref, state_ref, g_ref = refs
    i = pl.program_id(0)
    j2 = lax.rem(jnp.maximum(i - 1, 0), nc)

    @pl.when(i == 0)
    def _():
        g_ref[1] = jnp.zeros(g_ref.shape[1:], F32)
        if need_out:
            gate_ref[1] = jnp.zeros(gate_ref.shape[1:], BF16)

    @pl.when(j2 == 0)
    def _():
        state_ref[...] = s0_ref[0]

    def silu(x):
        return x * jax.nn.sigmoid(x)

    def step(wr, rd):
        u = _modulated_input(h1_ref[0], mod1_ref, gain_ref)

        def store_g(val):
            g_ref[wr] = val

        low_rank, logit, cumsum = _gla_decay_units(False, u, w_ref, wup_ref, bup_ref, store_g)

        def gate(lo):
            act = silu if lo < 2 * VW else jax.nn.sigmoid

            def run():
                gate_ref[wr, :, lo:lo + SLICE_W] = act(_dot(u, w_ref[:, lo:lo + SLICE_W])).astype(BF16)
            return run

        slices = [gate(lo) for lo in range(0, QKV_W, SLICE_W)] if need_out else []

        acc = {}

        def head_norm(x, center):
            if center:
                x = x - jnp.mean(x, axis=-1, keepdims=True)
            return x * lax.rsqrt(jnp.mean(x * x, axis=-1, keepdims=True) + EPS)

        outs = {}

        def sink(branch, h, o):
            outs[branch, h] = o

        def branch_proj(branch, h, cols=slice(0, D_MODEL)):
            def run():
                vs = slice(h * DV, (h + 1) * DV)
                off = 0 if branch == "ret" else VW
                gs = slice(off + h * DV, off + (h + 1) * DV)
                key = ("y", branch, h)
                gain_row, w_ = (rng_ref, wbr_ref) if branch == "ret" else (gng_ref, wbg_ref)
                if key not in outs:
                    x = head_norm(of_ref[0, :, gs] + outs.pop((branch, h)), branch == "ret") * gain_row[:, vs]
                    outs[key] = (x * gate_ref[rd, :, gs].astype(F32)).astype(BF16)
                part = _dot(outs[key], w_[vs, cols])
                k = (branch, cols.start)
                acc[k] = part if k not in acc else acc[k] + part
            return run

        half = D_MODEL // 2

        def out_proj(lo):
            def run():
                cols = slice(lo, lo + half)
                merged = (gate_ref[rd, :, 2 * VW + lo:2 * VW + lo + half].astype(F32) * acc.pop(("ret", lo))
                          + gate_ref[rd, :, 3 * VW + lo:3 * VW + lo + half].astype(F32) * acc.pop(("gla", lo)))
                part = _dot(merged.astype(BF16), wo_ref[cols, :])
                acc["out"] = part if "out" not in acc else acc["out"] + part
            return run

        units = _scan_units(
            False,
            qkv_ref[0, :, 0:QK].astype(F32), qkv_ref[0, :, QK:V_OFF].astype(F32), qkv_ref[0, :, V_OFF:G_OFF],
            qkv_ref[0, :, G_OFF:G_OFF + QK].astype(F32), qkv_ref[0, :, G_OFF + QK:G_OFF + 2 * QK].astype(F32),
            qkv_ref[0, :, G_OFF + 2 * QK:QKV_W],
            g_ref[rd], _log_gamma_row(rd_ref), state_ref, sink)

        if not need_out:
            _run(_scan_order([low_rank, logit, cumsum]), units)
            return

        lo_, hi_ = slice(0, half), slice(half, D_MODEL)
        bp = lambda b, h: [branch_proj(b, h, lo_), branch_proj(b, h, hi_)]
        fill = [
            [low_rank, slices[0]],
            [logit, slices[1]],
            [cumsum, slices[2]] + bp("ret", 0),
            [slices[3]] + bp("ret", 1),
            [slices[4]] + bp("ret", 2),
            [slices[5]] + bp("ret", 3) + bp("gla", 0),
            bp("gla", 1),
            [slices[6]] + bp("gla", 2) + [slices[7]] + bp("gla", 3) + [out_proj(0), out_proj(half)],
        ]
        _run(_scan_order(_grouped(fill)), units)

        h_new = h2_ref[0] + mod2_ref[0, 2:3, :] * acc.pop("out")
        if final_norm:
            ms = jnp.mean(h_new * h_new, axis=-1, keepdims=True)
            h_new = h_new * lax.rsqrt(ms + EPS) * fng_ref[...]
        hout_ref[0] = h_new

    _pipelined(i, step)

    @pl.when(j2 == nc - 1)
    def _():
        sout_ref[0] = state_ref[...]


def _backward_sweep(h, mod, gain, w2, qkv, o_fwd, rd_row, wup, bup, ret_gain, gla_gain,
                    w_br, w_bg, w_o, final_gain, s0, *, need_out, final_norm):
    b, n_tok, _ = h.shape
    nc = n_tok // CHUNK
    n = b * nc

    def block_of(k):
        return (k // nc) * nc + (nc - 1 - k % nc)

    k1 = lambda i: jnp.minimum(i, n - 1)
    k2 = lambda i: jnp.maximum(i - 1, 0)
    tok1 = pl.BlockSpec((1, CHUNK, D_MODEL), lambda i: (block_of(k1(i)), 0, 0))
    tok2 = pl.BlockSpec((1, CHUNK, D_MODEL), lambda i: (block_of(k2(i)), 0, 0))
    state_spec = pl.BlockSpec((1, 2 * HEADS, DK, DV), lambda i: (k2(i) // nc, 0, 0, 0))
    out_specs = [state_spec]
    out_shape = [jax.ShapeDtypeStruct((b, 2 * HEADS, DK, DV), F32)]
    scratch = [pltpu.VMEM((2 * HEADS, DK, DV), F32), pltpu.VMEM((2, CHUNK, QK), F32)]
    if need_out:
        out_specs = [tok2] + out_specs
        out_shape = [jax.ShapeDtypeStruct((n, CHUNK, D_MODEL), F32)] + out_shape
        scratch = scratch + [pltpu.VMEM((2, CHUNK, QKV_W), BF16)]
    hb = h.reshape(n, CHUNK, D_MODEL)
    outs = pl.pallas_call(
        functools.partial(_bwd_kernel, nc=nc, need_out=need_out, final_norm=final_norm),
        grid=(n + 1,),
        in_specs=[
            tok1,
            tok2,
            pl.BlockSpec((1, 3, D_MODEL), lambda i: (k1(i) // nc, 0, 0)),
            pl.BlockSpec((1, 3, D_MODEL), lambda i: (k2(i) // nc, 0, 0)),
            _const_spec((1, D_MODEL)),
            _const_spec((D_MODEL, HALF_W)),
            pl.BlockSpec((1, CHUNK, QKV_W), lambda i: (block_of(k2(i)), 0, 0)),
            pl.BlockSpec((1, CHUNK, 2 * VW), lambda i: (block_of(k2(i)), 0, 0)),
            _const_spec((1, QK)),
            _const_spec((RANK_PAD, QK)),
            _const_spec((1, QK)),
            _const_spec((1, VW)),
            _const_spec((1, VW)),
            _const_spec((VW, D_MODEL)),
            _const_spec((VW, D_MODEL)),
            _const_spec((D_MODEL, D_MODEL)),
            _const_spec((1, D_MODEL)),
            state_spec,
        ],
        out_specs=out_specs,
        out_shape=out_shape,
        scratch_shapes=scratch,
        compiler_params=pltpu.CompilerParams(
            dimension_semantics=("arbitrary",), vmem_limit_bytes=VMEM_LIMIT_BYTES),
        name="backward_sweep",
    )(hb, hb, mod, mod, gain, w2, qkv, o_fwd, rd_row, wup, bup, ret_gain, gla_gain, w_br, w_bg, w_o,
      final_gain, s0)
    if need_out:
        return outs[0].reshape(b, n_tok, D_MODEL), outs[1]
    return None, outs[0]


def _rotary_tables(n_rows):
    r_idx, c_idx = jnp.meshgrid(jnp.arange(n_rows), jnp.arange(GRID_W), indexing="ij")
    r_idx = r_idx.reshape(-1).astype(F32)
    c_idx = c_idx.reshape(-1).astype(F32)
    n_freq = DK // 4
    inv_freq = ROPE_BASE ** (-jnp.arange(n_freq, dtype=F32) / n_freq)
    ang_r = r_idx[:, None] * inv_freq
    ang_c = c_idx[:, None] * inv_freq
    ang = jnp.stack([ang_r, ang_r, ang_c, ang_c], axis=1).reshape(-1, DK)
    cos, sin = jnp.cos(ang), jnp.sin(ang)
    first = (jnp.arange(DK) // n_freq) % 2 == 0
    return cos, jnp.where(first, -sin, 0.0), jnp.where(first, 0.0, sin)


def _split_w_in(w):
    idx = [0]
    for width in (QK, QK, VW, VW, QK, QK, VW, VW, RANK, D_MODEL, D_MODEL):
        idx.append(idx[-1] + width)
    blk = [w[:, idx[i]:idx[i + 1]] for i in range(11)]
    rq, rk, rv, rg, gq, gk, gv, gg, glr, ma, mb = blk
    glr = jnp.pad(glr, ((0, 0), (0, RANK_PAD - RANK)))
    w1 = jnp.concatenate([rq, rk, rv, gq, gk, gv, glr], axis=1).astype(BF16)
    w2 = jnp.concatenate([rg, gg, ma, mb, glr], axis=1).astype(BF16)
    return w1, w2


def kernel(x, c, ctx, c_ctx, norm_gain, w_ada, b_ada, w_in, ret_decay, gla_w_up, gla_b_up, ret_norm_gain,
           gla_norm_gain, w_branch_ret, w_branch_gla, w_out, final_norm_gain):
    batch, n_lat, _ = x.shape
    n_ctx = ctx.shape[1]
    depth = w_in.shape[0]

    cvecs = jnp.concatenate([c, c_ctx[None, :], jnp.zeros((16 - batch - 1, D_MODEL), F32)], axis=0)
    mods = _modulation(cvecs, w_ada, b_ada)

    rot_lat = _rotary_tables(n_lat // GRID_W)
    rot_ctx = (jnp.ones((n_ctx, DK), F32), jnp.zeros((n_ctx, DK), F32), jnp.zeros((n_ctx, DK), F32))
    zero_state = jnp.zeros((batch, 2 * HEADS, DK, DV), F32)
    fgain = final_norm_gain.reshape(1, D_MODEL)

    h_lat, h_ctx = x, ctx
    for l in range(depth):
        last = l == depth - 1
        mod_lat = mods[l, :batch].reshape(batch, 3, D_MODEL)
        mod_ctx = jnp.broadcast_to(mods[l, batch].reshape(1, 3, D_MODEL), (batch, 3, D_MODEL))
        gain = norm_gain[l].reshape(1, D_MODEL)
        w1, w2 = _split_w_in(w_in[l])
        rd_f = jnp.repeat(ret_decay[l, 0], DK).reshape(1, QK)
        rd_b = jnp.repeat(ret_decay[l, 1], DK).reshape(1, QK)
        wup = jnp.pad(gla_w_up[l], ((0, 0), (0, RANK_PAD - RANK), (0, 0))).astype(BF16)
        bup = gla_b_up[l].reshape(2, 1, QK)
        tail = (rd_b, wup[1], bup[1], ret_norm_gain[l].reshape(1, VW), gla_norm_gain[l].reshape(1, VW),
                w_branch_ret[l].astype(BF16), w_branch_gla[l].astype(BF16), w_out[l].astype(BF16), fgain)

        qkv_c, of_c, sf_c = _forward_sweep(h_ctx, mod_ctx, gain, w1, rot_ctx, rd_f, wup[0], bup[0], zero_state)
        qkv_l, of_l, _ = _forward_sweep(h_lat, mod_lat, gain, w1, rot_lat, rd_f, wup[0], bup[0], sf_c)
        h_ctx_new, sb_c = _backward_sweep(h_ctx, mod_ctx, gain, w2, qkv_c, of_c, *tail, zero_state,
                                          need_out=not last, final_norm=False)
        h_lat, _ = _backward_sweep(h_lat, mod_lat, gain, w2, qkv_l, of_l, *tail, sb_c,
                                   need_out=True, final_norm=last)
        h_ctx = h_ctx_new
    return h_lat
```

```python
import functools

import jax
import jax.numpy as jnp
from jax import lax
from jax.experimental import pallas as pl
from jax.experimental.pallas import tpu as pltpu

D_MODEL = 1024
HEADS = 4
DK = 128
DV = 256
QK = HEADS * DK
VW = HEADS * DV
RANK = 16
RANK_PAD = 128
CHUNK = 128
GLA_CHUNK = 64
LOGIT_NORM = 16.0
GRID_W = 64
ROPE_BASE = 10000.0
EPS = 1e-6

QKV_W = 2 * (2 * QK + VW)
HALF_W = QKV_W + RANK_PAD
SLICE_W = 512
V_OFF = 2 * QK
G_OFF = 2 * QK + VW
VMEM_LIMIT_BYTES = 56 * 1024 * 1024

F32 = jnp.float32
BF16 = jnp.bfloat16


def _dot(a, b):
    return jnp.dot(a, b, preferred_element_type=F32)


def _dot_nt(a, b):
    return lax.dot_general(a, b, (((1,), (1,)), ((), ())), preferred_element_type=F32)


def _dot_tn(a, b):
    return lax.dot_general(a, b, (((0,), (0,)), ((), ())), preferred_element_type=F32)


def _mod_kernel(c_ref, w_ref, b_ref, o_ref):
    c = c_ref[...]
    s = c * jax.nn.sigmoid(c)
    o_ref[0] = jnp.dot(s, w_ref[0], preferred_element_type=F32,
                       precision=lax.Precision.HIGHEST) + b_ref[0]


def _modulation(cvecs, w_ada, b_ada):
    depth = w_ada.shape[0]
    n = cvecs.shape[0]
    return pl.pallas_call(
        _mod_kernel,
        grid=(depth, 3),
        in_specs=[
            pl.BlockSpec((n, D_MODEL), lambda l, j: (0, 0)),
            pl.BlockSpec((1, D_MODEL, D_MODEL), lambda l, j: (l, 0, j)),
            pl.BlockSpec((1, 1, D_MODEL), lambda l, j: (l, 0, j)),
        ],
        out_specs=pl.BlockSpec((1, n, D_MODEL), lambda l, j: (l, 0, j)),
        out_shape=jax.ShapeDtypeStruct((depth, n, 3 * D_MODEL), F32),
        name="adaln_modulation",
    )(cvecs, w_ada, b_ada.reshape(depth, 1, 3 * D_MODEL))


def _modulated_input(h, mod_ref, gain_ref):
    ms = jnp.mean(h * h, axis=-1, keepdims=True)
    row_gain = gain_ref[...] * (1.0 + mod_ref[0, 1:2, :])
    u = h * lax.rsqrt(ms + EPS) * row_gain + mod_ref[0, 0:1, :]
    return u.astype(BF16)


def _log_sigmoid(x):
    return jnp.minimum(x, 0.0) - jnp.log(1.0 + jnp.exp(-jnp.abs(x)))


def _gla_decay_units(u, w_ref, wup_ref, bup_ref, tri_ref, store):
    hold = {}

    def low_rank():
        hold["glr"] = _dot(u, w_ref[:, QKV_W:HALF_W]).astype(BF16)

    def logit():
        x = _dot(hold.pop("glr"), wup_ref[...]) + bup_ref[...]
        log_a = _log_sigmoid(x) * (1.0 / LOGIT_NORM)
        hi = log_a.astype(BF16)
        lo = (log_a - hi.astype(F32)).astype(BF16)
        hold["hilo"] = jnp.concatenate([hi, lo], axis=0)

    def cumsum():
        store(_dot(tri_ref[...], hold.pop("hilo")))

    return low_rank, logit, cumsum


def _cumsum_matrix(fwd):
    r = jnp.arange(CHUNK)[:, None]
    c = jnp.arange(CHUNK)[None, :]
    same = (r // GLA_CHUNK) == (c // GLA_CHUNK)
    tri = jnp.where(same & ((c <= r) if fwd else (c >= r)), 1.0, 0.0).astype(BF16)
    return jnp.concatenate([tri, tri], axis=1)


def _store_retention_tables(fwd, lg_row, qk_ref, mask_ref):
    C = CHUNK
    row = lax.broadcasted_iota(jnp.int32, (C, QK), 0)
    col = lax.broadcasted_iota(jnp.int32, (C, QK), 1) & (C - 1)
    diff = (row - col) if fwd else (col - row)
    keep = (diff >= 0) if fwd else (diff > 0)
    mask_ref[...] = jnp.where(keep, jnp.exp(lg_row * jnp.where(keep, diff, 0).astype(F32)), 0.0)
    posf = row.astype(F32)
    if fwd:
        qexp, kexp = posf + 1.0, (C - 1.0) - posf
    else:
        qexp, kexp = C - posf, posf
    qk_ref[0] = jnp.exp(lg_row * qexp).astype(BF16)
    qk_ref[1] = jnp.exp(lg_row * kexp).astype(BF16)


def _scan_units(fwd, rq, rk, rv, gq, gk, gv, g, lg_row, qk_ref, mask_ref, state_ref, sink):
    C = CHUNK
    row = lax.broadcasted_iota(jnp.int32, (C, 2 * C), 0)
    col = lax.broadcasted_iota(jnp.int32, (C, 2 * C), 1) & (C - 1)
    q_dec = rq * qk_ref[0]
    k_dec = rk * qk_ref[1]
    gam = jnp.exp(lg_row * float(C))

    G = GLA_CHUNK
    ref_i, last_i = (G // 2 - 1, G - 1) if fwd else (G // 2, 0)

    def rows_of(i):
        top = jnp.broadcast_to(g[i:i + 1, :], (G, QK))
        bot = jnp.broadcast_to(g[G + i:G + i + 1, :], (G, QK))
        return jnp.concatenate([top, bot], axis=0)

    g_ref_rows = rows_of(ref_i)
    g_last_rows = rows_of(last_i)
    last_top = g[last_i:last_i + 1, :]
    last_bot = g[G + last_i:G + last_i + 1, :]
    last_p, last_q = (last_top, last_bot) if fwd else (last_bot, last_top)
    is_top = lax.broadcasted_iota(jnp.int32, (C, QK), 0) < G
    in_p = is_top if fwd else jnp.logical_not(is_top)
    q_rel = gq * jnp.exp(g - g_ref_rows).astype(BF16)
    k_rel = gk * jnp.exp(g_ref_rows - g).astype(BF16)
    q_loc = gq * jnp.exp(g).astype(BF16)
    k_end = jnp.exp(g_last_rows - g)
    k_upd = gk * (k_end * jnp.where(in_p, jnp.exp(last_q), 1.0)).astype(BF16)
    k_st = gk * jnp.where(in_p, k_end, 0.0).astype(BF16)
    q_in = gq * jnp.exp(g + jnp.where(in_p, 0.0, last_p)).astype(BF16)
    d_tot = jnp.exp(last_p + last_q)
    same = (row >> 6) == (col >> 6)
    keep_g = same & ((col <= row) if fwd else (col > row))
    q_rows = slice(G, C) if fwd else slice(0, G)

    units, hold = {}, {}

    def pair_scores(q, k, p):
        a = k[:, 2 * p * DK:(2 * p + 1) * DK]
        b = k[:, (2 * p + 1) * DK:(2 * p + 2) * DK]
        z = jnp.zeros_like(a)
        k_bd = jnp.concatenate([jnp.concatenate([a, z], axis=1), jnp.concatenate([z, b], axis=1)], axis=0)
        return _dot_nt(q[:, 2 * p * DK:(2 * p + 2) * DK], k_bd)

    def add_pair(p):
        lanes = slice(2 * p * DK, (2 * p + 2) * DK)

        def ret_s():
            s = (pair_scores(rq, rk, p) * mask_ref[:, lanes]).astype(BF16)
            hold["rs", 2 * p], hold["rs", 2 * p + 1] = s[:, :C], s[:, C:]

        def gla_s():
            local = pair_scores(q_rel, k_rel, p)
            xq = pair_scores(q_loc[q_rows, :], k_st, p)
            zeros_half = jnp.zeros((G, 2 * C), F32)
            xfull = jnp.concatenate([zeros_half, xq] if fwd else [xq, zeros_half], axis=0)
            s = jnp.where(keep_g, local, xfull).astype(BF16)
            hold["gs", 2 * p], hold["gs", 2 * p + 1] = s[:, :C], s[:, C:]

        units.update({("ret", "s", p): ret_s, ("gla", "s", p): gla_s})

    def add(h):
        sl = slice(h * DK, (h + 1) * DK)
        vs = slice(h * DV, (h + 1) * DV)

        def ret_kv():
            hold["rkv", h] = _dot_tn(k_dec[:, sl], rv[:, vs])

        def ret_o():
            s_prev = state_ref[h]
            lhs = jnp.concatenate([hold.pop(("rs", h)), q_dec[:, sl]], axis=1)
            o = _dot(lhs, jnp.concatenate([rv[:, vs], s_prev.astype(BF16)], axis=0))
            state_ref[h] = jnp.concatenate([gam[:, sl], gam[:, sl]], axis=1) * s_prev + hold.pop(("rkv", h))
            sink("ret", h, o)

        def gla_kv():
            hold["gkv", h] = _dot_tn(k_upd[:, sl], gv[:, vs])

        def gla_o():
            s_prev = state_ref[HEADS + h]
            lhs = jnp.concatenate([hold.pop(("gs", h)), q_in[:, sl]], axis=1)
            o = _dot(lhs, jnp.concatenate([gv[:, vs], s_prev.astype(BF16)], axis=0))
            dcol = jnp.broadcast_to(d_tot[:, sl], (DK, DK)).T
            state_ref[HEADS + h] = jnp.concatenate([dcol, dcol], axis=1) * s_prev + hold.pop(("gkv", h))
            sink("gla", h, o)

        units.update({("ret", "kv", h): ret_kv, ("ret", "o", h): ret_o,
                      ("gla", "kv", h): gla_kv, ("gla", "o", h): gla_o})

    for p in range(HEADS // 2):
        add_pair(p)
    for h in range(HEADS):
        add(h)
    return units


def _log_gamma_row(rd_ref):
    return jnp.log1p(-jnp.exp(rd_ref[...]))


def _pipelined(i, step):
    parity = lax.rem(i, 2)

    @pl.when(parity == 0)
    def _():
        step(0, 1)

    @pl.when(parity == 1)
    def _():
        step(1, 0)


def _scan_order(fill):
    r, g = "ret", "gla"
    f = [list(x) for x in fill]
    return ([(r, "s", 0), (r, "s", 1)] + f[0]
            + [(r, "kv", 0), (r, "kv", 1), (r, "o", 0)] + f[1]
            + [(r, "o", 1), (r, "kv", 2), (r, "kv", 3), (g, "s", 0)] + f[2]
            + [(r, "o", 2), (r, "o", 3), (g, "kv", 0), (g, "kv", 1)] + f[3]
            + [(g, "o", 0), (g, "s", 1)] + f[4]
            + [(g, "o", 1), (g, "kv", 2), (g, "kv", 3)] + f[5]
            + [(g, "o", 2)] + f[6] + [(g, "o", 3)] + f[7])


def _run(order, units):
    for item in order:
        if callable(item):
            item()
        else:
            units[item]()


def _fwd_kernel(h_ref, mod_ref, gain_ref, w_ref, cos_ref, sina_ref, sinb_ref, rd_ref, wup_ref, bup_ref,
                tri_ref, s0_ref, qkv_ref, of_ref, sout_ref, state_ref, stage_ref, g_ref, qk_ref, mask_ref, *, nc):
    i = pl.program_id(0)
    j2 = lax.rem(jnp.maximum(i - 1, 0), nc)

    @pl.when(i == 0)
    def _():
        stage_ref[1] = jnp.zeros(stage_ref.shape[1:], BF16)
        g_ref[1] = jnp.zeros(g_ref.shape[1:], F32)
        _store_retention_tables(True, _log_gamma_row(rd_ref), qk_ref, mask_ref)

    @pl.when(j2 == 0)
    def _():
        state_ref[...] = s0_ref[0]

    def step(wr, rd):
        u = _modulated_input(h_ref[0], mod_ref, gain_ref)
        cos, sina, sinb = cos_ref[...], sina_ref[...], sinb_ref[...]

        def rotary(t):
            parts = []
            for h in range(HEADS):
                th = t[:, h * DK:(h + 1) * DK]
                parts.append(th * cos + pltpu.roll(th, 96, 1) * sina + pltpu.roll(th, 32, 1) * sinb)
            return jnp.concatenate(parts, axis=1)

        def proj(lo, post):
            def run():
                val = post(_dot(u, w_ref[:, lo:lo + SLICE_W])).astype(BF16)
                qkv_ref[0, :, lo:lo + SLICE_W] = val
                stage_ref[wr, :, lo:lo + SLICE_W] = val
            return run

        scale = DK ** -0.5
        post = {0: lambda t: rotary(t * scale), QK: rotary, G_OFF: lambda t: t * scale}
        slices = [proj(lo, post.get(lo, lambda t: t)) for lo in range(0, QKV_W, SLICE_W)]

        def store_g(val):
            g_ref[wr] = val

        low_rank, logit, cumsum = _gla_decay_units(u, w_ref, wup_ref, bup_ref, tri_ref, store_g)

        def sink(branch, h, o):
            off = (0 if branch == "ret" else VW) + h * DV
            of_ref[0, :, off:off + DV] = o

        units = _scan_units(
            True,
            stage_ref[rd, :, 0:QK], stage_ref[rd, :, QK:V_OFF], stage_ref[rd, :, V_OFF:G_OFF],
            stage_ref[rd, :, G_OFF:G_OFF + QK], stage_ref[rd, :, G_OFF + QK:G_OFF + 2 * QK],
            stage_ref[rd, :, G_OFF + 2 * QK:QKV_W],
            g_ref[rd], _log_gamma_row(rd_ref), qk_ref, mask_ref, state_ref, sink)

        fill = [[low_rank, slices[0]], [logit, slices[1]], [cumsum, slices[2]]] + [[s] for s in slices[3:]]
        _run(_scan_order(fill), units)

    _pipelined(i, step)

    @pl.when(j2 == nc - 1)
    def _():
        sout_ref[0] = state_ref[...]


def _const_spec(shape):
    return pl.BlockSpec(shape, lambda i: (0,) * len(shape), pipeline_mode=pl.Buffered(1))


def _forward_sweep(h, mod, gain, w1, rot, rd_row, wup, bup, s0):
    b, n_tok, _ = h.shape
    nc = n_tok // CHUNK
    n = b * nc
    c1 = lambda i: jnp.minimum(i, n - 1)
    c2 = lambda i: jnp.maximum(i - 1, 0)
    return pl.pallas_call(
        functools.partial(_fwd_kernel, nc=nc),
        grid=(n + 1,),
        in_specs=[
            pl.BlockSpec((1, CHUNK, D_MODEL), lambda i: (c1(i), 0, 0)),
            pl.BlockSpec((1, 3, D_MODEL), lambda i: (c1(i) // nc, 0, 0)),
            _const_spec((1, D_MODEL)),
            _const_spec((D_MODEL, HALF_W)),
            pl.BlockSpec((CHUNK, DK), lambda i: (c1(i) % nc, 0)),
            pl.BlockSpec((CHUNK, DK), lambda i: (c1(i) % nc, 0)),
            pl.BlockSpec((CHUNK, DK), lambda i: (c1(i) % nc, 0)),
            _const_spec((1, QK)),
            _const_spec((RANK_PAD, QK)),
            _const_spec((1, QK)),
            _const_spec((CHUNK, 2 * CHUNK)),
            pl.BlockSpec((1, 2 * HEADS, DK, DV), lambda i: (c2(i) // nc, 0, 0, 0)),
        ],
        out_specs=[
            pl.BlockSpec((1, CHUNK, QKV_W), lambda i: (c1(i), 0, 0)),
            pl.BlockSpec((1, CHUNK, 2 * VW), lambda i: (c2(i), 0, 0)),
            pl.BlockSpec((1, 2 * HEADS, DK, DV), lambda i: (c2(i) // nc, 0, 0, 0)),
        ],
        out_shape=[
            jax.ShapeDtypeStruct((n, CHUNK, QKV_W), BF16),
            jax.ShapeDtypeStruct((n, CHUNK, 2 * VW), F32),
            jax.ShapeDtypeStruct((b, 2 * HEADS, DK, DV), F32),
        ],
        scratch_shapes=[
            pltpu.VMEM((2 * HEADS, DK, DV), F32),
            pltpu.VMEM((2, CHUNK, QKV_W), BF16),
            pltpu.VMEM((2, CHUNK, QK), F32),
            pltpu.VMEM((2, CHUNK, QK), BF16),
            pltpu.VMEM((CHUNK, QK), F32),
        ],
        compiler_params=pltpu.CompilerParams(
            dimension_semantics=("arbitrary",), vmem_limit_bytes=VMEM_LIMIT_BYTES),
        name="forward_sweep",
    )(h.reshape(n, CHUNK, D_MODEL), mod, gain, w1, *rot, rd_row, wup, bup, _cumsum_matrix(True), s0)


def _bwd_kernel(h1_ref, h2_ref, mod1_ref, mod2_ref, gain_ref, w_ref, qkv_ref, of_ref, rd_ref, wup_ref, bup_ref,
                tri_ref, rng_ref, gng_ref, wbr_ref, wbg_ref, wo_ref, fng_ref, s0_ref,
                *refs, nc, need_out, final_norm):
    if need_out:
        hout_ref, sout_ref, state_ref, g_ref, qk_ref, mask_ref, gate_ref = refs
    else:
        sout_ref, state_ref, g_ref, qk_ref, mask_ref = refs
    i = pl.program_id(0)
    j2 = lax.rem(jnp.maximum(i - 1, 0), nc)

    @pl.when(i == 0)
    def _():
        g_ref[1] = jnp.zeros(g_ref.shape[1:], F32)
        _store_retention_tables(False, _log_gamma_row(rd_ref), qk_ref, mask_ref)
        if need_out:
            gate_ref[1] = jnp.zeros(gate_ref.shape[1:], BF16)

    @pl.when(j2 == 0)
    def _():
        state_ref[...] = s0_ref[0]

    def silu(x):
        return x * jax.nn.sigmoid(x)

    def step(wr, rd):
        u = _modulated_input(h1_ref[0], mod1_ref, gain_ref)

        def store_g(val):
            g_ref[wr] = val

        low_rank, logit, cumsum = _gla_decay_units(u, w_ref, wup_ref, bup_ref, tri_ref, store_g)

        def gate(lo):
            act = silu if lo < 2 * VW else jax.nn.sigmoid

            def run():
                gate_ref[wr, :, lo:lo + SLICE_W] = act(_dot(u, w_ref[:, lo:lo + SLICE_W])).astype(BF16)
            return run

        slices = [gate(lo) for lo in range(0, QKV_W, SLICE_W)] if need_out else []

        acc = {}

        def head_norm(x, center):
            if center:
                x = x - jnp.mean(x, axis=-1, keepdims=True)
            return x * lax.rsqrt(jnp.mean(x * x, axis=-1, keepdims=True) + EPS)

        outs = {}

        def sink(branch, h, o):
            outs[branch, h] = o

        def branch_proj(branch, p):
            def run():
                gain_row, w_ = (rng_ref, wbr_ref) if branch == "ret" else (gng_ref, wbg_ref)
                off = 0 if branch == "ret" else VW
                ys = []
                for h in (2 * p, 2 * p + 1):
                    vs = slice(h * DV, (h + 1) * DV)
                    gs = slice(off + h * DV, off + (h + 1) * DV)
                    x = head_norm(of_ref[0, :, gs] + outs.pop((branch, h)), branch == "ret") * gain_row[:, vs]
                    ys.append((x * gate_ref[rd, :, gs].astype(F32)).astype(BF16))
                part = _dot(jnp.concatenate(ys, axis=1), w_[2 * p * DV:(2 * p + 2) * DV, :])
                acc[branch] = part if branch not in acc else acc[branch] + part
            return run

        def out_proj():
            merged = (gate_ref[rd, :, 2 * VW:3 * VW].astype(F32) * acc.pop("ret")
                      + gate_ref[rd, :, 3 * VW:QKV_W].astype(F32) * acc.pop("gla"))
            acc["out"] = _dot(merged.astype(BF16), wo_ref[...])

        units = _scan_units(
            False,
            qkv_ref[0, :, 0:QK], qkv_ref[0, :, QK:V_OFF], qkv_ref[0, :, V_OFF:G_OFF],
            qkv_ref[0, :, G_OFF:G_OFF + QK], qkv_ref[0, :, G_OFF + QK:G_OFF + 2 * QK],
            qkv_ref[0, :, G_OFF + 2 * QK:QKV_W],
            g_ref[rd], _log_gamma_row(rd_ref), qk_ref, mask_ref, state_ref, sink)

        if not need_out:
            _run(_scan_order([[low_rank], [logit], [cumsum]] + [[]] * 5), units)
            return

        fill = [
            [low_rank, slices[0], slices[1]],
            [logit, slices[2]],
            [cumsum, slices[3]],
            [slices[4]],
            [branch_proj("ret", 0)],
            [slices[5]],
            [branch_proj("ret", 1)],
            [branch_proj("gla", 0), slices[6], branch_proj("gla", 1), slices[7], out_proj],
        ]
        _run(_scan_order(fill), units)

        h_new = h2_ref[0] + mod2_ref[0, 2:3, :] * acc.pop("out")
        if final_norm:
            ms = jnp.mean(h_new * h_new, axis=-1, keepdims=True)
            h_new = h_new * lax.rsqrt(ms + EPS) * fng_ref[...]
        hout_ref[0] = h_new

    _pipelined(i, step)

    @pl.when(j2 == nc - 1)
    def _():
        sout_ref[0] = state_ref[...]


def _backward_sweep(h, mod, gain, w2, qkv, o_fwd, rd_row, wup, bup, ret_gain, gla_gain,
                    w_br, w_bg, w_o, final_gain, s0, *, need_out, final_norm):
    b, n_tok, _ = h.shape
    nc = n_tok // CHUNK
    n = b * nc

    def block_of(k):
        return (k // nc) * nc + (nc - 1 - k % nc)

    k1 = lambda i: jnp.minimum(i, n - 1)
    k2 = lambda i: jnp.maximum(i - 1, 0)
    tok1 = pl.BlockSpec((1, CHUNK, D_MODEL), lambda i: (block_of(k1(i)), 0, 0))
    tok2 = pl.BlockSpec((1, CHUNK, D_MODEL), lambda i: (block_of(k2(i)), 0, 0))
    state_spec = pl.BlockSpec((1, 2 * HEADS, DK, DV), lambda i: (k2(i) // nc, 0, 0, 0))
    out_specs = [state_spec]
    out_shape = [jax.ShapeDtypeStruct((b, 2 * HEADS, DK, DV), F32)]
    scratch = [pltpu.VMEM((2 * HEADS, DK, DV), F32), pltpu.VMEM((2, CHUNK, QK), F32),
               pltpu.VMEM((2, CHUNK, QK), BF16), pltpu.VMEM((CHUNK, QK), F32)]
    if need_out:
        out_specs = [tok2] + out_specs
        out_shape = [jax.ShapeDtypeStruct((n, CHUNK, D_MODEL), F32)] + out_shape
        scratch = scratch + [pltpu.VMEM((2, CHUNK, QKV_W), BF16)]
    hb = h.reshape(n, CHUNK, D_MODEL)
    outs = pl.pallas_call(
        functools.partial(_bwd_kernel, nc=nc, need_out=need_out, final_norm=final_norm),
        grid=(n + 1,),
        in_specs=[
            tok1,
            tok2,
            pl.BlockSpec((1, 3, D_MODEL), lambda i: (k1(i) // nc, 0, 0)),
            pl.BlockSpec((1, 3, D_MODEL), lambda i: (k2(i) // nc, 0, 0)),
            _const_spec((1, D_MODEL)),
            _const_spec((D_MODEL, HALF_W)),
            pl.BlockSpec((1, CHUNK, QKV_W), lambda i: (block_of(k2(i)), 0, 0)),
            pl.BlockSpec((1, CHUNK, 2 * VW), lambda i: (block_of(k2(i)), 0, 0)),
            _const_spec((1, QK)),
            _const_spec((RANK_PAD, QK)),
            _const_spec((1, QK)),
            _const_spec((CHUNK, 2 * CHUNK)),
            _const_spec((1, VW)),
            _const_spec((1, VW)),
            _const_spec((VW, D_MODEL)),
            _const_spec((VW, D_MODEL)),
            _const_spec((D_MODEL, D_MODEL)),
            _const_spec((1, D_MODEL)),
            state_spec,
        ],
        out_specs=out_specs,
        out_shape=out_shape,
        scratch_shapes=scratch,
        compiler_params=pltpu.CompilerParams(
            dimension_semantics=("arbitrary",), vmem_limit_bytes=VMEM_LIMIT_BYTES),
        name="backward_sweep",
    )(hb, hb, mod, mod, gain, w2, qkv, o_fwd, rd_row, wup, bup, _cumsum_matrix(False), ret_gain, gla_gain,
      w_br, w_bg, w_o, final_gain, s0)
    if need_out:
        return outs[0].reshape(b, n_tok, D_MODEL), outs[1]
    return None, outs[0]


def _rotary_tables(n_rows):
    r_idx, c_idx = jnp.meshgrid(jnp.arange(n_rows), jnp.arange(GRID_W), indexing="ij")
    r_idx = r_idx.reshape(-1).astype(F32)
    c_idx = c_idx.reshape(-1).astype(F32)
    n_freq = DK // 4
    inv_freq = ROPE_BASE ** (-jnp.arange(n_freq, dtype=F32) / n_freq)
    ang_r = r_idx[:, None] * inv_freq
    ang_c = c_idx[:, None] * inv_freq
    ang = jnp.stack([ang_r, ang_r, ang_c, ang_c], axis=1).reshape(-1, DK)
    cos, sin = jnp.cos(ang), jnp.sin(ang)
    first = (jnp.arange(DK) // n_freq) % 2 == 0
    return cos, jnp.where(first, -sin, 0.0), jnp.where(first, 0.0, sin)


def _split_w_in(w):
    idx = [0]
    for width in (QK, QK, VW, VW, QK, QK, VW, VW, RANK, D_MODEL, D_MODEL):
        idx.append(idx[-1] + width)
    blk = [w[:, idx[i]:idx[i + 1]] for i in range(11)]
    rq, rk, rv, rg, gq, gk, gv, gg, glr, ma, mb = blk
    glr = jnp.pad(glr, ((0, 0), (0, RANK_PAD - RANK)))
    w1 = jnp.concatenate([rq, rk, rv, gq, gk, gv, glr], axis=1).astype(BF16)
    w2 = jnp.concatenate([rg, gg, ma, mb, glr], axis=1).astype(BF16)
    return w1, w2


def kernel(x, c, ctx, c_ctx, norm_gain, w_ada, b_ada, w_in, ret_decay, gla_w_up, gla_b_up, ret_norm_gain,
           gla_norm_gain, w_branch_ret, w_branch_gla, w_out, final_norm_gain):
    batch, n_lat, _ = x.shape
    n_ctx = ctx.shape[1]
    depth = w_in.shape[0]

    cvecs = jnp.concatenate([c, c_ctx[None, :], jnp.zeros((16 - batch - 1, D_MODEL), F32)], axis=0)
    mods = _modulation(cvecs, w_ada, b_ada)

    rot_lat = _rotary_tables(n_lat // GRID_W)
    rot_ctx = (jnp.ones((n_ctx, DK), F32), jnp.zeros((n_ctx, DK), F32), jnp.zeros((n_ctx, DK), F32))
    zero_state = jnp.zeros((batch, 2 * HEADS, DK, DV), F32)
    fgain = final_norm_gain.reshape(1, D_MODEL)

    h_lat, h_ctx = x, ctx
    for l in range(depth):
        last = l == depth - 1
        mod_lat = mods[l, :batch].reshape(batch, 3, D_MODEL)
        mod_ctx = jnp.broadcast_to(mods[l, batch].reshape(1, 3, D_MODEL), (batch, 3, D_MODEL))
        gain = norm_gain[l].reshape(1, D_MODEL)
        w1, w2 = _split_w_in(w_in[l])
        rd_f = jnp.repeat(ret_decay[l, 0], DK).reshape(1, QK)
        rd_b = jnp.repeat(ret_decay[l, 1], DK).reshape(1, QK)
        wup = jnp.pad(gla_w_up[l], ((0, 0), (0, RANK_PAD - RANK), (0, 0))).astype(BF16)
        bup = gla_b_up[l].reshape(2, 1, QK)
        tail = (rd_b, wup[1], bup[1], ret_norm_gain[l].reshape(1, VW), gla_norm_gain[l].reshape(1, VW),
                w_branch_ret[l].astype(BF16), w_branch_gla[l].astype(BF16), w_out[l].astype(BF16), fgain)

        qkv_c, of_c, sf_c = _forward_sweep(h_ctx, mod_ctx, gain, w1, rot_ctx, rd_f, wup[0], bup[0], zero_state)
        qkv_l, of_l, _ = _forward_sweep(h_lat, mod_lat, gain, w1, rot_lat, rd_f, wup[0], bup[0], sf_c)
        h_ctx_new, sb_c = _backward_sweep(h_ctx, mod_ctx, gain, w2, qkv_c, of_c, *tail, zero_state,
                                          need_out=not last, final_norm=False)
        h_lat, _ = _backward_sweep(h_lat, mod_lat, gain, w2, qkv_l, of_l, *tail, sb_c,
                                   need_out=True, final_norm=last)
        h_ctx = h_ctx_new
    return h_lat
```

```python
import functools

import jax
import jax.numpy as jnp
from jax import lax
from jax.experimental import pallas as pl
from jax.experimental.pallas import tpu as pltpu

D_MODEL = 1024
HEADS = 4
DK = 128
DV = 256
QK = HEADS * DK
VW = HEADS * DV
RANK = 16
RANK_PAD = 128
CHUNK = 128
HALVES = 2
BLOCK = CHUNK * HALVES
GLA_CHUNK = 64
LOGIT_NORM = 16.0
GRID_W = 64
ROPE_BASE = 10000.0
EPS = 1e-6

QKV_W = 2 * (2 * QK + VW)
HALF_W = QKV_W + RANK_PAD
SLICE_W = 512
V_OFF = 2 * QK
G_OFF = 2 * QK + VW
VMEM_LIMIT_BYTES = 56 * 1024 * 1024

F32 = jnp.float32
BF16 = jnp.bfloat16


def _dot(a, b):
    return jnp.dot(a, b, preferred_element_type=F32)


def _dot_nt(a, b):
    return lax.dot_general(a, b, (((1,), (1,)), ((), ())), preferred_element_type=F32)


def _dot_tn(a, b):
    return lax.dot_general(a, b, (((0,), (0,)), ((), ())), preferred_element_type=F32)


def _mod_kernel(c_ref, w_ref, b_ref, o_ref):
    c = c_ref[...]
    s = c * jax.nn.sigmoid(c)
    o_ref[0] = jnp.dot(s, w_ref[0], preferred_element_type=F32,
                       precision=lax.Precision.HIGHEST) + b_ref[0]


def _modulation(cvecs, w_ada, b_ada):
    depth = w_ada.shape[0]
    n = cvecs.shape[0]
    return pl.pallas_call(
        _mod_kernel,
        grid=(depth, 3),
        in_specs=[
            pl.BlockSpec((n, D_MODEL), lambda l, j: (0, 0)),
            pl.BlockSpec((1, D_MODEL, D_MODEL), lambda l, j: (l, 0, j)),
            pl.BlockSpec((1, 1, D_MODEL), lambda l, j: (l, 0, j)),
        ],
        out_specs=pl.BlockSpec((1, n, D_MODEL), lambda l, j: (l, 0, j)),
        out_shape=jax.ShapeDtypeStruct((depth, n, 3 * D_MODEL), F32),
        name="adaln_modulation",
    )(cvecs, w_ada, b_ada.reshape(depth, 1, 3 * D_MODEL))


def _modulated_input(h, mod_ref, gain_ref):
    ms = jnp.mean(h * h, axis=-1, keepdims=True)
    row_gain = gain_ref[...] * (1.0 + mod_ref[0, 1:2, :])
    u = h * lax.rsqrt(ms + EPS) * row_gain + mod_ref[0, 0:1, :]
    return u.astype(BF16)


def _log_sigmoid(x):
    return jnp.minimum(x, 0.0) - jnp.log(1.0 + jnp.exp(-jnp.abs(x)))


def _gla_decay_units(u, w_ref, wup_ref, bup_ref, tri_ref, store):
    hold = {}

    def low_rank():
        hold["glr"] = _dot(u, w_ref[:, QKV_W:HALF_W]).astype(BF16)

    def logit():
        x = _dot(hold.pop("glr"), wup_ref[...]) + bup_ref[...]
        log_a = _log_sigmoid(x) * (1.0 / LOGIT_NORM)
        hi = log_a.astype(BF16)
        lo = (log_a - hi.astype(F32)).astype(BF16)
        hold["hilo"] = jnp.concatenate([hi, lo], axis=0)

    def cumsum():
        store(_dot(tri_ref[...], hold.pop("hilo")))

    return low_rank, logit, cumsum


def _cumsum_matrix(fwd):
    r = jnp.arange(CHUNK)[:, None]
    c = jnp.arange(CHUNK)[None, :]
    same = (r // GLA_CHUNK) == (c // GLA_CHUNK)
    tri = jnp.where(same & ((c <= r) if fwd else (c >= r)), 1.0, 0.0).astype(BF16)
    return jnp.concatenate([tri, tri], axis=1)


def _store_retention_tables(fwd, lg_row, qk_ref, mask_ref):
    C = CHUNK
    row = lax.broadcasted_iota(jnp.int32, (C, QK), 0)
    col = lax.broadcasted_iota(jnp.int32, (C, QK), 1) & (C - 1)
    diff = (row - col) if fwd else (col - row)
    keep = (diff >= 0) if fwd else (diff > 0)
    mask_ref[...] = jnp.where(keep, jnp.exp(lg_row * jnp.where(keep, diff, 0).astype(F32)), 0.0)
    posf = row.astype(F32)
    if fwd:
        qexp, kexp = posf + 1.0, (C - 1.0) - posf
    else:
        qexp, kexp = C - posf, posf
    qk_ref[0] = jnp.exp(lg_row * qexp).astype(BF16)
    qk_ref[1] = jnp.exp(lg_row * kexp).astype(BF16)


def _scan_units(fwd, rq, rk, rv, gq, gk, gv, g, lg_row, qk_ref, mask_ref, state_ref, sink):
    C = CHUNK
    row = lax.broadcasted_iota(jnp.int32, (C, 2 * C), 0)
    col = lax.broadcasted_iota(jnp.int32, (C, 2 * C), 1) & (C - 1)
    q_dec = rq * qk_ref[0]
    k_dec = rk * qk_ref[1]
    gam = jnp.exp(lg_row * float(C))

    G = GLA_CHUNK
    ref_i, last_i = (G // 2 - 1, G - 1) if fwd else (G // 2, 0)

    def rows_of(i):
        top = jnp.broadcast_to(g[i:i + 1, :], (G, QK))
        bot = jnp.broadcast_to(g[G + i:G + i + 1, :], (G, QK))
        return jnp.concatenate([top, bot], axis=0)

    g_ref_rows = rows_of(ref_i)
    g_last_rows = rows_of(last_i)
    last_top = g[last_i:last_i + 1, :]
    last_bot = g[G + last_i:G + last_i + 1, :]
    last_p, last_q = (last_top, last_bot) if fwd else (last_bot, last_top)
    is_top = lax.broadcasted_iota(jnp.int32, (C, QK), 0) < G
    in_p = is_top if fwd else jnp.logical_not(is_top)
    q_rel = gq * jnp.exp(g - g_ref_rows).astype(BF16)
    k_rel = gk * jnp.exp(g_ref_rows - g).astype(BF16)
    q_loc = gq * jnp.exp(g).astype(BF16)
    k_end = jnp.exp(g_last_rows - g)
    k_upd = gk * (k_end * jnp.where(in_p, jnp.exp(last_q), 1.0)).astype(BF16)
    k_st = gk * jnp.where(in_p, k_end, 0.0).astype(BF16)
    q_in = gq * jnp.exp(g + jnp.where(in_p, 0.0, last_p)).astype(BF16)
    d_tot = jnp.exp(last_p + last_q)
    same = (row >> 6) == (col >> 6)
    keep_g = same & ((col <= row) if fwd else (col > row))
    q_rows = slice(G, C) if fwd else slice(0, G)

    units, hold = {}, {}

    def pair_scores(q, k, p):
        a = k[:, 2 * p * DK:(2 * p + 1) * DK]
        b = k[:, (2 * p + 1) * DK:(2 * p + 2) * DK]
        z = jnp.zeros_like(a)
        k_bd = jnp.concatenate([jnp.concatenate([a, z], axis=1), jnp.concatenate([z, b], axis=1)], axis=0)
        return _dot_nt(q[:, 2 * p * DK:(2 * p + 2) * DK], k_bd)

    def add_pair(p):
        lanes = slice(2 * p * DK, (2 * p + 2) * DK)

        def ret_s():
            s = (pair_scores(rq, rk, p) * mask_ref[:, lanes]).astype(BF16)
            hold["rs", 2 * p], hold["rs", 2 * p + 1] = s[:, :C], s[:, C:]

        def gla_s():
            local = pair_scores(q_rel, k_rel, p)
            xq = pair_scores(q_loc[q_rows, :], k_st, p)
            zeros_half = jnp.zeros((G, 2 * C), F32)
            xfull = jnp.concatenate([zeros_half, xq] if fwd else [xq, zeros_half], axis=0)
            s = jnp.where(keep_g, local, xfull).astype(BF16)
            hold["gs", 2 * p], hold["gs", 2 * p + 1] = s[:, :C], s[:, C:]

        units.update({("ret", "s", p): ret_s, ("gla", "s", p): gla_s})

    def add(h):
        sl = slice(h * DK, (h + 1) * DK)
        vs = slice(h * DV, (h + 1) * DV)

        def ret_kv():
            hold["rkv", h] = _dot_tn(k_dec[:, sl], rv[:, vs])

        def ret_o():
            s_prev = state_ref[h]
            lhs = jnp.concatenate([hold.pop(("rs", h)), q_dec[:, sl]], axis=1)
            o = _dot(lhs, jnp.concatenate([rv[:, vs], s_prev.astype(BF16)], axis=0))
            state_ref[h] = jnp.concatenate([gam[:, sl], gam[:, sl]], axis=1) * s_prev + hold.pop(("rkv", h))
            sink("ret", h, o)

        def gla_kv():
            hold["gkv", h] = _dot_tn(k_upd[:, sl], gv[:, vs])

        def gla_o():
            s_prev = state_ref[HEADS + h]
            lhs = jnp.concatenate([hold.pop(("gs", h)), q_in[:, sl]], axis=1)
            o = _dot(lhs, jnp.concatenate([gv[:, vs], s_prev.astype(BF16)], axis=0))
            dcol = jnp.broadcast_to(d_tot[:, sl], (DK, DK)).T
            state_ref[HEADS + h] = jnp.concatenate([dcol, dcol], axis=1) * s_prev + hold.pop(("gkv", h))
            sink("gla", h, o)

        units.update({("ret", "kv", h): ret_kv, ("ret", "o", h): ret_o,
                      ("gla", "kv", h): gla_kv, ("gla", "o", h): gla_o})

    for p in range(HEADS // 2):
        add_pair(p)
    for h in range(HEADS):
        add(h)
    return units


def _log_gamma_row(rd_ref):
    return jnp.log1p(-jnp.exp(rd_ref[...]))


def _pipelined(i, step):
    parity = lax.rem(i, 2)

    @pl.when(parity == 0)
    def _():
        step(0, HALVES)

    @pl.when(parity == 1)
    def _():
        step(HALVES, 0)


def _zero_slots(ref, first):
    for s in range(first, first + HALVES):
        ref[s] = jnp.zeros(ref.shape[1:], ref.dtype)


def _scan_order(fill):
    r, g = "ret", "gla"
    f = [list(x) for x in fill]
    return ([(r, "s", 0), (r, "s", 1)] + f[0]
            + [(r, "kv", 0), (r, "kv", 1), (r, "o", 0)] + f[1]
            + [(r, "o", 1), (r, "kv", 2), (r, "kv", 3), (g, "s", 0)] + f[2]
            + [(r, "o", 2), (r, "o", 3), (g, "kv", 0), (g, "kv", 1)] + f[3]
            + [(g, "o", 0), (g, "s", 1)] + f[4]
            + [(g, "o", 1), (g, "kv", 2), (g, "kv", 3)] + f[5]
            + [(g, "o", 2)] + f[6] + [(g, "o", 3)] + f[7])


def _run(order, units):
    for item in order:
        if callable(item):
            item()
        else:
            units[item]()


def _fwd_kernel(h_ref, mod_ref, gain_ref, w_ref, cos_ref, sina_ref, sinb_ref, rd_ref, wup_ref, bup_ref,
                tri_ref, s0_ref, qkv_ref, of_ref, sout_ref, state_ref, stage_ref, g_ref, qk_ref, mask_ref, *, nb):
    i = pl.program_id(0)
    j2 = lax.rem(jnp.maximum(i - 1, 0), nb)

    @pl.when(i == 0)
    def _():
        _zero_slots(stage_ref, HALVES)
        _zero_slots(g_ref, HALVES)
        _store_retention_tables(True, _log_gamma_row(rd_ref), qk_ref, mask_ref)

    @pl.when(j2 == 0)
    def _():
        state_ref[...] = s0_ref[0]

    def half_step(half, wr, rd):
        rows = slice(half * CHUNK, (half + 1) * CHUNK)
        u = _modulated_input(h_ref[0, rows, :], mod_ref, gain_ref)
        cos, sina, sinb = cos_ref[rows, :], sina_ref[rows, :], sinb_ref[rows, :]

        def rotary(t):
            parts = []
            for h in range(HEADS):
                th = t[:, h * DK:(h + 1) * DK]
                parts.append(th * cos + pltpu.roll(th, 96, 1) * sina + pltpu.roll(th, 32, 1) * sinb)
            return jnp.concatenate(parts, axis=1)

        def proj(lo, post):
            def run():
                val = post(_dot(u, w_ref[:, lo:lo + SLICE_W])).astype(BF16)
                qkv_ref[0, rows, lo:lo + SLICE_W] = val
                stage_ref[wr, :, lo:lo + SLICE_W] = val
            return run

        scale = DK ** -0.5
        post = {0: lambda t: rotary(t * scale), QK: rotary, G_OFF: lambda t: t * scale}
        slices = [proj(lo, post.get(lo, lambda t: t)) for lo in range(0, QKV_W, SLICE_W)]

        def store_g(val):
            g_ref[wr] = val

        low_rank, logit, cumsum = _gla_decay_units(u, w_ref, wup_ref, bup_ref, tri_ref, store_g)

        def sink(branch, h, o):
            off = (0 if branch == "ret" else VW) + h * DV
            of_ref[0, rows, off:off + DV] = o

        units = _scan_units(
            True,
            stage_ref[rd, :, 0:QK], stage_ref[rd, :, QK:V_OFF], stage_ref[rd, :, V_OFF:G_OFF],
            stage_ref[rd, :, G_OFF:G_OFF + QK], stage_ref[rd, :, G_OFF + QK:G_OFF + 2 * QK],
            stage_ref[rd, :, G_OFF + 2 * QK:QKV_W],
            g_ref[rd], _log_gamma_row(rd_ref), qk_ref, mask_ref, state_ref, sink)

        fill = [[low_rank, slices[0]], [logit, slices[1]], [cumsum, slices[2]]] + [[s] for s in slices[3:]]
        _run(_scan_order(fill), units)

    def step(wr0, rd0):
        for half in range(HALVES):
            half_step(half, wr0 + half, rd0 + half)

    _pipelined(i, step)

    @pl.when(j2 == nb - 1)
    def _():
        sout_ref[0] = state_ref[...]


def _const_spec(shape):
    return pl.BlockSpec(shape, lambda i: (0,) * len(shape), pipeline_mode=pl.Buffered(1))


def _forward_sweep(h, mod, gain, w1, rot, rd_row, wup, bup, s0):
    b, n_tok, _ = h.shape
    nb = n_tok // BLOCK
    n = b * nb
    c1 = lambda i: jnp.minimum(i, n - 1)
    c2 = lambda i: jnp.maximum(i - 1, 0)
    return pl.pallas_call(
        functools.partial(_fwd_kernel, nb=nb),
        grid=(n + 1,),
        in_specs=[
            pl.BlockSpec((1, BLOCK, D_MODEL), lambda i: (c1(i), 0, 0)),
            pl.BlockSpec((1, 3, D_MODEL), lambda i: (c1(i) // nb, 0, 0)),
            _const_spec((1, D_MODEL)),
            _const_spec((D_MODEL, HALF_W)),
            pl.BlockSpec((BLOCK, DK), lambda i: (c1(i) % nb, 0)),
            pl.BlockSpec((BLOCK, DK), lambda i: (c1(i) % nb, 0)),
            pl.BlockSpec((BLOCK, DK), lambda i: (c1(i) % nb, 0)),
            _const_spec((1, QK)),
            _const_spec((RANK_PAD, QK)),
            _const_spec((1, QK)),
            _const_spec((CHUNK, 2 * CHUNK)),
            pl.BlockSpec((1, 2 * HEADS, DK, DV), lambda i: (c2(i) // nb, 0, 0, 0)),
        ],
        out_specs=[
            pl.BlockSpec((1, BLOCK, QKV_W), lambda i: (c1(i), 0, 0)),
            pl.BlockSpec((1, BLOCK, 2 * VW), lambda i: (c2(i), 0, 0)),
            pl.BlockSpec((1, 2 * HEADS, DK, DV), lambda i: (c2(i) // nb, 0, 0, 0)),
        ],
        out_shape=[
            jax.ShapeDtypeStruct((n, BLOCK, QKV_W), BF16),
            jax.ShapeDtypeStruct((n, BLOCK, 2 * VW), F32),
            jax.ShapeDtypeStruct((b, 2 * HEADS, DK, DV), F32),
        ],
        scratch_shapes=[
            pltpu.VMEM((2 * HEADS, DK, DV), F32),
            pltpu.VMEM((2 * HALVES, CHUNK, QKV_W), BF16),
            pltpu.VMEM((2 * HALVES, CHUNK, QK), F32),
            pltpu.VMEM((2, CHUNK, QK), BF16),
            pltpu.VMEM((CHUNK, QK), F32),
        ],
        compiler_params=pltpu.CompilerParams(
            dimension_semantics=("arbitrary",), vmem_limit_bytes=VMEM_LIMIT_BYTES),
        name="forward_sweep",
    )(h.reshape(n, BLOCK, D_MODEL), mod, gain, w1, *rot, rd_row, wup, bup, _cumsum_matrix(True), s0)


def _bwd_kernel(h1_ref, h2_ref, mod1_ref, mod2_ref, gain_ref, w_ref, qkv_ref, of_ref, rd_ref, wup_ref, bup_ref,
                tri_ref, rng_ref, gng_ref, wbr_ref, wbg_ref, wo_ref, fng_ref, s0_ref,
                *refs, nb, need_out, final_norm):
    if need_out:
        hout_ref, sout_ref, state_ref, g_ref, qk_ref, mask_ref, gate_ref = refs
    else:
        sout_ref, state_ref, g_ref, qk_ref, mask_ref = refs
    i = pl.program_id(0)
    j2 = lax.rem(jnp.maximum(i - 1, 0), nb)

    @pl.when(i == 0)
    def _():
        _zero_slots(g_ref, HALVES)
        _store_retention_tables(False, _log_gamma_row(rd_ref), qk_ref, mask_ref)
        if need_out:
            _zero_slots(gate_ref, HALVES)

    @pl.when(j2 == 0)
    def _():
        state_ref[...] = s0_ref[0]

    def silu(x):
        return x * jax.nn.sigmoid(x)

    def half_step(half, wr, rd):
        rows = slice(half * CHUNK, (half + 1) * CHUNK)
        u = _modulated_input(h1_ref[0, rows, :], mod1_ref, gain_ref)

        def store_g(val):
            g_ref[wr] = val

        low_rank, logit, cumsum = _gla_decay_units(u, w_ref, wup_ref, bup_ref, tri_ref, store_g)

        def gate(lo):
            act = silu if lo < 2 * VW else jax.nn.sigmoid

            def run():
                gate_ref[wr, :, lo:lo + SLICE_W] = act(_dot(u, w_ref[:, lo:lo + SLICE_W])).astype(BF16)
            return run

        slices = [gate(lo) for lo in range(0, QKV_W, SLICE_W)] if need_out else []

        acc = {}

        def head_norm(x, center):
            if center:
                x = x - jnp.mean(x, axis=-1, keepdims=True)
            return x * lax.rsqrt(jnp.mean(x * x, axis=-1, keepdims=True) + EPS)

        outs = {}

        def sink(branch, h, o):
            outs[branch, h] = o

        def branch_proj(branch, p):
            def run():
                gain_row, w_ = (rng_ref, wbr_ref) if branch == "ret" else (gng_ref, wbg_ref)
                off = 0 if branch == "ret" else VW
                ys = []
                for h in (2 * p, 2 * p + 1):
                    vs = slice(h * DV, (h + 1) * DV)
                    gs = slice(off + h * DV, off + (h + 1) * DV)
                    x = head_norm(of_ref[0, rows, gs] + outs.pop((branch, h)), branch == "ret") * gain_row[:, vs]
                    ys.append((x * gate_ref[rd, :, gs].astype(F32)).astype(BF16))
                part = _dot(jnp.concatenate(ys, axis=1), w_[2 * p * DV:(2 * p + 2) * DV, :])
                acc[branch] = part if branch not in acc else acc[branch] + part
            return run

        def out_proj():
            merged = (gate_ref[rd, :, 2 * VW:3 * VW].astype(F32) * acc.pop("ret")
                      + gate_ref[rd, :, 3 * VW:QKV_W].astype(F32) * acc.pop("gla"))
            acc["out"] = _dot(merged.astype(BF16), wo_ref[...])

        units = _scan_units(
            False,
            qkv_ref[0, rows, 0:QK], qkv_ref[0, rows, QK:V_OFF], qkv_ref[0, rows, V_OFF:G_OFF],
            qkv_ref[0, rows, G_OFF:G_OFF + QK], qkv_ref[0, rows, G_OFF + QK:G_OFF + 2 * QK],
            qkv_ref[0, rows, G_OFF + 2 * QK:QKV_W],
            g_ref[rd], _log_gamma_row(rd_ref), qk_ref, mask_ref, state_ref, sink)

        if not need_out:
            _run(_scan_order([[low_rank], [logit], [cumsum]] + [[]] * 5), units)
            return

        fill = [
            [low_rank, slices[0], slices[1]],
            [logit, slices[2]],
            [cumsum, slices[3]],
            [slices[4]],
            [branch_proj("ret", 0)],
            [slices[5]],
            [branch_proj("ret", 1)],
            [branch_proj("gla", 0), slices[6], branch_proj("gla", 1), slices[7], out_proj],
        ]
        _run(_scan_order(fill), units)

        h_new = h2_ref[0, rows, :] + mod2_ref[0, 2:3, :] * acc.pop("out")
        if final_norm:
            ms = jnp.mean(h_new * h_new, axis=-1, keepdims=True)
            h_new = h_new * lax.rsqrt(ms + EPS) * fng_ref[...]
        hout_ref[0, rows, :] = h_new

    def step(wr0, rd0):
        for half in reversed(range(HALVES)):
            half_step(half, wr0 + half, rd0 + half)

    _pipelined(i, step)

    @pl.when(j2 == nb - 1)
    def _():
        sout_ref[0] = state_ref[...]


def _backward_sweep(h, mod, gain, w2, qkv, o_fwd, rd_row, wup, bup, ret_gain, gla_gain,
                    w_br, w_bg, w_o, final_gain, s0, *, need_out, final_norm):
    b, n_tok, _ = h.shape
    nb = n_tok // BLOCK
    n = b * nb

    def block_of(k):
        return (k // nb) * nb + (nb - 1 - k % nb)

    k1 = lambda i: jnp.minimum(i, n - 1)
    k2 = lambda i: jnp.maximum(i - 1, 0)
    tok1 = pl.BlockSpec((1, BLOCK, D_MODEL), lambda i: (block_of(k1(i)), 0, 0))
    tok2 = pl.BlockSpec((1, BLOCK, D_MODEL), lambda i: (block_of(k2(i)), 0, 0))
    state_spec = pl.BlockSpec((1, 2 * HEADS, DK, DV), lambda i: (k2(i) // nb, 0, 0, 0))
    out_specs = [state_spec]
    out_shape = [jax.ShapeDtypeStruct((b, 2 * HEADS, DK, DV), F32)]
    scratch = [pltpu.VMEM((2 * HEADS, DK, DV), F32), pltpu.VMEM((2 * HALVES, CHUNK, QK), F32),
               pltpu.VMEM((2, CHUNK, QK), BF16), pltpu.VMEM((CHUNK, QK), F32)]
    if need_out:
        out_specs = [tok2] + out_specs
        out_shape = [jax.ShapeDtypeStruct((n, BLOCK, D_MODEL), F32)] + out_shape
        scratch = scratch + [pltpu.VMEM((2 * HALVES, CHUNK, QKV_W), BF16)]
    hb = h.reshape(n, BLOCK, D_MODEL)
    outs = pl.pallas_call(
        functools.partial(_bwd_kernel, nb=nb, need_out=need_out, final_norm=final_norm),
        grid=(n + 1,),
        in_specs=[
            tok1,
            tok2,
            pl.BlockSpec((1, 3, D_MODEL), lambda i: (k1(i) // nb, 0, 0)),
            pl.BlockSpec((1, 3, D_MODEL), lambda i: (k2(i) // nb, 0, 0)),
            _const_spec((1, D_MODEL)),
            _const_spec((D_MODEL, HALF_W)),
            pl.BlockSpec((1, BLOCK, QKV_W), lambda i: (block_of(k2(i)), 0, 0)),
            pl.BlockSpec((1, BLOCK, 2 * VW), lambda i: (block_of(k2(i)), 0, 0)),
            _const_spec((1, QK)),
            _const_spec((RANK_PAD, QK)),
            _const_spec((1, QK)),
            _const_spec((CHUNK, 2 * CHUNK)),
            _const_spec((1, VW)),
            _const_spec((1, VW)),
            _const_spec((VW, D_MODEL)),
            _const_spec((VW, D_MODEL)),
            _const_spec((D_MODEL, D_MODEL)),
            _const_spec((1, D_MODEL)),
            state_spec,
        ],
        out_specs=out_specs,
        out_shape=out_shape,
        scratch_shapes=scratch,
        compiler_params=pltpu.CompilerParams(
            dimension_semantics=("arbitrary",), vmem_limit_bytes=VMEM_LIMIT_BYTES),
        name="backward_sweep",
    )(hb, hb, mod, mod, gain, w2, qkv, o_fwd, rd_row, wup, bup, _cumsum_matrix(False), ret_gain, gla_gain,
      w_br, w_bg, w_o, final_gain, s0)
    if need_out:
        return outs[0].reshape(b, n_tok, D_MODEL), outs[1]
    return None, outs[0]


def _rotary_tables(n_rows):
    r_idx, c_idx = jnp.meshgrid(jnp.arange(n_rows), jnp.arange(GRID_W), indexing="ij")
    r_idx = r_idx.reshape(-1).astype(F32)
    c_idx = c_idx.reshape(-1).astype(F32)
    n_freq = DK // 4
    inv_freq = ROPE_BASE ** (-jnp.arange(n_freq, dtype=F32) / n_freq)
    ang_r = r_idx[:, None] * inv_freq
    ang_c = c_idx[:, None] * inv_freq
    ang = jnp.stack([ang_r, ang_r, ang_c, ang_c], axis=1).reshape(-1, DK)
    cos, sin = jnp.cos(ang), jnp.sin(ang)
    first = (jnp.arange(DK) // n_freq) % 2 == 0
    return cos, jnp.where(first, -sin, 0.0), jnp.where(first, 0.0, sin)


def _split_w_in(w):
    idx = [0]
    for width in (QK, QK, VW, VW, QK, QK, VW, VW, RANK, D_MODEL, D_MODEL):
        idx.append(idx[-1] + width)
    blk = [w[:, idx[i]:idx[i + 1]] for i in range(11)]
    rq, rk, rv, rg, gq, gk, gv, gg, glr, ma, mb = blk
    glr = jnp.pad(glr, ((0, 0), (0, RANK_PAD - RANK)))
    w1 = jnp.concatenate([rq, rk, rv, gq, gk, gv, glr], axis=1).astype(BF16)
    w2 = jnp.concatenate([rg, gg, ma, mb, glr], axis=1).astype(BF16)
    return w1, w2


def kernel(x, c, ctx, c_ctx, norm_gain, w_ada, b_ada, w_in, ret_decay, gla_w_up, gla_b_up, ret_norm_gain,
           gla_norm_gain, w_branch_ret, w_branch_gla, w_out, final_norm_gain):
    batch, n_lat, _ = x.shape
    n_ctx = ctx.shape[1]
    depth = w_in.shape[0]

    cvecs = jnp.concatenate([c, c_ctx[None, :], jnp.zeros((16 - batch - 1, D_MODEL), F32)], axis=0)
    mods = _modulation(cvecs, w_ada, b_ada)

    rot_lat = _rotary_tables(n_lat // GRID_W)
    rot_ctx = (jnp.ones((n_ctx, DK), F32), jnp.zeros((n_ctx, DK), F32), jnp.zeros((n_ctx, DK), F32))
    zero_state = jnp.zeros((batch, 2 * HEADS, DK, DV), F32)
    fgain = final_norm_gain.reshape(1, D_MODEL)

    h_lat, h_ctx = x, ctx
    for l in range(depth):
        last = l == depth - 1
        mod_lat = mods[l, :batch].reshape(batch, 3, D_MODEL)
        mod_ctx = jnp.broadcast_to(mods[l, batch].reshape(1, 3, D_MODEL), (batch, 3, D_MODEL))
        gain = norm_gain[l].reshape(1, D_MODEL)
        w1, w2 = _split_w_in(w_in[l])
        rd_f = jnp.repeat(ret_decay[l, 0], DK).reshape(1, QK)
        rd_b = jnp.repeat(ret_decay[l, 1], DK).reshape(1, QK)
        wup = jnp.pad(gla_w_up[l], ((0, 0), (0, RANK_PAD - RANK), (0, 0))).astype(BF16)
        bup = gla_b_up[l].reshape(2, 1, QK)
        tail = (rd_b, wup[1], bup[1], ret_norm_gain[l].reshape(1, VW), gla_norm_gain[l].reshape(1, VW),
                w_branch_ret[l].astype(BF16), w_branch_gla[l].astype(BF16), w_out[l].astype(BF16), fgain)

        qkv_c, of_c, sf_c = _forward_sweep(h_ctx, mod_ctx, gain, w1, rot_ctx, rd_f, wup[0], bup[0], zero_state)
        qkv_l, of_l, _ = _forward_sweep(h_lat, mod_lat, gain, w1, rot_lat, rd_f, wup[0], bup[0], sf_c)
        h_ctx_new, sb_c = _backward_sweep(h_ctx, mod_ctx, gain, w2, qkv_c, of_c, *tail, zero_state,
                                          need_out=not last, final_norm=False)
        h_lat, _ = _backward_sweep(h_lat, mod_lat, gain, w2, qkv_l, of_l, *tail, sb_c,
                                   need_out=True, final_norm=last)
        h_ctx = h_ctx_new
    return h_lat
```

```python
import functools

import jax
import jax.numpy as jnp
from jax import lax
from jax.experimental import pallas as pl
from jax.experimental.pallas import tpu as pltpu

D_MODEL = 1024
HEADS = 4
DK = 128
DV = 256
QK = HEADS * DK
VW = HEADS * DV
RANK = 16
RANK_PAD = 128
CHUNK = 128
HALVES = 2
BLOCK = CHUNK * HALVES
GLA_CHUNK = 64
LOGIT_NORM = 16.0
GRID_W = 64
ROPE_BASE = 10000.0
EPS = 1e-6

QKV_W = 2 * (2 * QK + VW)
HALF_W = QKV_W + RANK_PAD
SLICE_W = 512
V_OFF = 2 * QK
G_OFF = 2 * QK + VW
VMEM_LIMIT_BYTES = 56 * 1024 * 1024

F32 = jnp.float32
BF16 = jnp.bfloat16


def _dot(a, b):
    return jnp.dot(a, b, preferred_element_type=F32)


def _dot_nt(a, b):
    return lax.dot_general(a, b, (((1,), (1,)), ((), ())), preferred_element_type=F32)


def _dot_tn(a, b):
    return lax.dot_general(a, b, (((0,), (0,)), ((), ())), preferred_element_type=F32)


def _mod_kernel(c_ref, w_ref, b_ref, o_ref):
    c = c_ref[...]
    s = c * jax.nn.sigmoid(c)
    o_ref[0] = jnp.dot(s, w_ref[0], preferred_element_type=F32,
                       precision=lax.Precision.HIGHEST) + b_ref[0]


def _modulation(cvecs, w_ada, b_ada):
    depth = w_ada.shape[0]
    n = cvecs.shape[0]
    return pl.pallas_call(
        _mod_kernel,
        grid=(depth, 3),
        in_specs=[
            pl.BlockSpec((n, D_MODEL), lambda l, j: (0, 0)),
            pl.BlockSpec((1, D_MODEL, D_MODEL), lambda l, j: (l, 0, j)),
            pl.BlockSpec((1, 1, D_MODEL), lambda l, j: (l, 0, j)),
        ],
        out_specs=pl.BlockSpec((1, n, D_MODEL), lambda l, j: (l, 0, j)),
        out_shape=jax.ShapeDtypeStruct((depth, n, 3 * D_MODEL), F32),
        name="adaln_modulation",
    )(cvecs, w_ada, b_ada.reshape(depth, 1, 3 * D_MODEL))


def _modulated_input(h, mod_ref, gain_ref):
    ms = jnp.mean(h * h, axis=-1, keepdims=True)
    row_gain = gain_ref[...] * (1.0 + mod_ref[0, 1:2, :])
    u = h * lax.rsqrt(ms + EPS) * row_gain + mod_ref[0, 0:1, :]
    return u.astype(BF16)


def _log_sigmoid(x):
    return jnp.minimum(x, 0.0) - jnp.log(1.0 + jnp.exp(-jnp.abs(x)))


def _gla_decay_units(u, w_ref, wup_ref, bup_ref, tri_ref, store):
    hold = {}

    def low_rank():
        hold["glr"] = _dot(u, w_ref[:, QKV_W:HALF_W]).astype(BF16)

    def logit():
        x = _dot(hold.pop("glr"), wup_ref[...]) + bup_ref[...]
        log_a = _log_sigmoid(x) * (1.0 / LOGIT_NORM)
        hi = log_a.astype(BF16)
        lo = (log_a - hi.astype(F32)).astype(BF16)
        hold["hilo"] = jnp.concatenate([hi, lo], axis=0)

    def cumsum():
        store(_dot(tri_ref[...], hold.pop("hilo")))

    return low_rank, logit, cumsum


def _cumsum_matrix(fwd):
    r = jnp.arange(CHUNK)[:, None]
    c = jnp.arange(CHUNK)[None, :]
    same = (r // GLA_CHUNK) == (c // GLA_CHUNK)
    tri = jnp.where(same & ((c <= r) if fwd else (c >= r)), 1.0, 0.0).astype(BF16)
    return jnp.concatenate([tri, tri], axis=1)


def _store_retention_tables(fwd, lg_row, qk_ref, mask_ref):
    C = CHUNK
    row = lax.broadcasted_iota(jnp.int32, (C, QK), 0)
    col = lax.broadcasted_iota(jnp.int32, (C, QK), 1) & (C - 1)
    diff = (row - col) if fwd else (col - row)
    keep = (diff >= 0) if fwd else (diff > 0)
    mask_ref[...] = jnp.where(keep, jnp.exp(lg_row * jnp.where(keep, diff, 0).astype(F32)), 0.0)
    posf = row.astype(F32)
    if fwd:
        qexp, kexp = posf + 1.0, (C - 1.0) - posf
    else:
        qexp, kexp = C - posf, posf
    qk_ref[0] = jnp.exp(lg_row * qexp).astype(BF16)
    qk_ref[1] = jnp.exp(lg_row * kexp).astype(BF16)


def _scan_units(fwd, rq, rk, rv, gq, gk, gv, g, lg_row, qk_ref, mask_ref, state_ref, sink):
    C = CHUNK
    row = lax.broadcasted_iota(jnp.int32, (C, 2 * C), 0)
    col = lax.broadcasted_iota(jnp.int32, (C, 2 * C), 1) & (C - 1)
    q_dec = rq * qk_ref[0]
    k_dec = rk * qk_ref[1]
    gam = jnp.exp(lg_row * float(C))

    G = GLA_CHUNK
    ref_i, last_i = (G // 2 - 1, G - 1) if fwd else (G // 2, 0)

    def rows_of(i):
        top = jnp.broadcast_to(g[i:i + 1, :], (G, QK))
        bot = jnp.broadcast_to(g[G + i:G + i + 1, :], (G, QK))
        return jnp.concatenate([top, bot], axis=0)

    g_ref_rows = rows_of(ref_i)
    g_last_rows = rows_of(last_i)
    last_top = g[last_i:last_i + 1, :]
    last_bot = g[G + last_i:G + last_i + 1, :]
    last_p, last_q = (last_top, last_bot) if fwd else (last_bot, last_top)
    is_top = lax.broadcasted_iota(jnp.int32, (C, QK), 0) < G
    in_p = is_top if fwd else jnp.logical_not(is_top)
    q_rel = gq * jnp.exp(g - g_ref_rows).astype(BF16)
    k_rel = gk * jnp.exp(g_ref_rows - g).astype(BF16)
    q_loc = gq * jnp.exp(g).astype(BF16)
    k_end = jnp.exp(g_last_rows - g)
    k_upd = gk * (k_end * jnp.where(in_p, jnp.exp(last_q), 1.0)).astype(BF16)
    k_st = gk * jnp.where(in_p, k_end, 0.0).astype(BF16)
    q_in = gq * jnp.exp(g + jnp.where(in_p, 0.0, last_p)).astype(BF16)
    d_tot = jnp.exp(last_p + last_q)
    same = (row >> 6) == (col >> 6)
    keep_g = same & ((col <= row) if fwd else (col > row))
    q_rows = slice(G, C) if fwd else slice(0, G)

    units, hold = {}, {}

    def pair_scores(q, k, p):
        a = k[:, 2 * p * DK:(2 * p + 1) * DK]
        b = k[:, (2 * p + 1) * DK:(2 * p + 2) * DK]
        z = jnp.zeros_like(a)
        k_bd = jnp.concatenate([jnp.concatenate([a, z], axis=1), jnp.concatenate([z, b], axis=1)], axis=0)
        return _dot_nt(q[:, 2 * p * DK:(2 * p + 2) * DK], k_bd)

    def add_pair(p):
        lanes = slice(2 * p * DK, (2 * p + 2) * DK)

        def ret_s():
            s = (pair_scores(rq, rk, p) * mask_ref[:, lanes]).astype(BF16)
            hold["rs", 2 * p], hold["rs", 2 * p + 1] = s[:, :C], s[:, C:]

        def gla_s():
            local = pair_scores(q_rel, k_rel, p)
            xq = pair_scores(q_loc[q_rows, :], k_st, p)
            zeros_half = jnp.zeros((G, 2 * C), F32)
            xfull = jnp.concatenate([zeros_half, xq] if fwd else [xq, zeros_half], axis=0)
            s = jnp.where(keep_g, local, xfull).astype(BF16)
            hold["gs", 2 * p], hold["gs", 2 * p + 1] = s[:, :C], s[:, C:]

        units.update({("ret", "s", p): ret_s, ("gla", "s", p): gla_s})

    def add(h):
        sl = slice(h * DK, (h + 1) * DK)
        vs = slice(h * DV, (h + 1) * DV)

        def ret_kv():
            hold["rkv", h] = _dot_tn(k_dec[:, sl], rv[:, vs])

        def ret_o():
            s_prev = state_ref[h]
            lhs = jnp.concatenate([hold.pop(("rs", h)), q_dec[:, sl]], axis=1)
            o = _dot(lhs, jnp.concatenate([rv[:, vs], s_prev.astype(BF16)], axis=0))
            state_ref[h] = jnp.concatenate([gam[:, sl], gam[:, sl]], axis=1) * s_prev + hold.pop(("rkv", h))
            sink("ret", h, o)

        def gla_kv():
            hold["gkv", h] = _dot_tn(k_upd[:, sl], gv[:, vs])

        def gla_o():
            s_prev = state_ref[HEADS + h]
            lhs = jnp.concatenate([hold.pop(("gs", h)), q_in[:, sl]], axis=1)
            o = _dot(lhs, jnp.concatenate([gv[:, vs], s_prev.astype(BF16)], axis=0))
            dcol = jnp.broadcast_to(d_tot[:, sl], (DK, DK)).T
            state_ref[HEADS + h] = jnp.concatenate([dcol, dcol], axis=1) * s_prev + hold.pop(("gkv", h))
            sink("gla", h, o)

        units.update({("ret", "kv", h): ret_kv, ("ret", "o", h): ret_o,
                      ("gla", "kv", h): gla_kv, ("gla", "o", h): gla_o})

    for p in range(HEADS // 2):
        add_pair(p)
    for h in range(HEADS):
        add(h)
    return units


def _log_gamma_row(rd_ref):
    return jnp.log1p(-jnp.exp(rd_ref[...]))


def _pipelined(i, step):
    parity = lax.rem(i, 2)

    @pl.when(parity == 0)
    def _():
        step(0, HALVES)

    @pl.when(parity == 1)
    def _():
        step(HALVES, 0)


def _zero_slots(ref, first):
    for s in range(first, first + HALVES):
        ref[s] = jnp.zeros(ref.shape[1:], ref.dtype)


def _scan_order(fill):
    r, g = "ret", "gla"
    f = [list(x) for x in fill]
    return ([(r, "s", 0), (r, "s", 1)] + f[0]
            + [(r, "kv", 0), (r, "kv", 1), (r, "o", 0)] + f[1]
            + [(r, "o", 1), (r, "kv", 2), (r, "kv", 3), (g, "s", 0)] + f[2]
            + [(r, "o", 2), (r, "o", 3), (g, "kv", 0), (g, "kv", 1)] + f[3]
            + [(g, "o", 0), (g, "s", 1)] + f[4]
            + [(g, "o", 1), (g, "kv", 2), (g, "kv", 3)] + f[5]
            + [(g, "o", 2)] + f[6] + [(g, "o", 3)] + f[7])


def _run(order, units):
    for item in order:
        if callable(item):
            item()
        else:
            units[item]()


def _fwd_kernel(h_ref, mod_ref, gain_ref, w_ref, cos_ref, sina_ref, sinb_ref, rd_ref, wup_ref, bup_ref,
                tri_ref, s0_ref, qkv_ref, u_ref, of_ref, sout_ref, state_ref, stage_ref, g_ref, qk_ref, mask_ref,
                *, nb):
    i = pl.program_id(0)
    j2 = lax.rem(jnp.maximum(i - 1, 0), nb)

    @pl.when(i == 0)
    def _():
        _zero_slots(stage_ref, HALVES)
        _zero_slots(g_ref, HALVES)
        _store_retention_tables(True, _log_gamma_row(rd_ref), qk_ref, mask_ref)

    @pl.when(j2 == 0)
    def _():
        state_ref[...] = s0_ref[0]

    def half_step(half, wr, rd):
        rows = slice(half * CHUNK, (half + 1) * CHUNK)
        u = _modulated_input(h_ref[0, rows, :], mod_ref, gain_ref)
        u_ref[0, rows, :] = u
        cos, sina, sinb = cos_ref[rows, :], sina_ref[rows, :], sinb_ref[rows, :]

        def rotary(t):
            parts = []
            for h in range(HEADS):
                th = t[:, h * DK:(h + 1) * DK]
                parts.append(th * cos + pltpu.roll(th, 96, 1) * sina + pltpu.roll(th, 32, 1) * sinb)
            return jnp.concatenate(parts, axis=1)

        def proj(lo, post):
            def run():
                val = post(_dot(u, w_ref[:, lo:lo + SLICE_W])).astype(BF16)
                qkv_ref[0, rows, lo:lo + SLICE_W] = val
                stage_ref[wr, :, lo:lo + SLICE_W] = val
            return run

        scale = DK ** -0.5
        post = {0: lambda t: rotary(t * scale), QK: rotary, G_OFF: lambda t: t * scale}
        slices = [proj(lo, post.get(lo, lambda t: t)) for lo in range(0, QKV_W, SLICE_W)]

        def store_g(val):
            g_ref[wr] = val

        low_rank, logit, cumsum = _gla_decay_units(u, w_ref, wup_ref, bup_ref, tri_ref, store_g)

        def sink(branch, h, o):
            off = (0 if branch == "ret" else VW) + h * DV
            of_ref[0, rows, off:off + DV] = o

        units = _scan_units(
            True,
            stage_ref[rd, :, 0:QK], stage_ref[rd, :, QK:V_OFF], stage_ref[rd, :, V_OFF:G_OFF],
            stage_ref[rd, :, G_OFF:G_OFF + QK], stage_ref[rd, :, G_OFF + QK:G_OFF + 2 * QK],
            stage_ref[rd, :, G_OFF + 2 * QK:QKV_W],
            g_ref[rd], _log_gamma_row(rd_ref), qk_ref, mask_ref, state_ref, sink)

        fill = [[low_rank, slices[0]], [logit, slices[1]], [cumsum, slices[2]]] + [[s] for s in slices[3:]]
        _run(_scan_order(fill), units)

    def step(wr0, rd0):
        for half in range(HALVES):
            half_step(half, wr0 + half, rd0 + half)

    _pipelined(i, step)

    @pl.when(j2 == nb - 1)
    def _():
        sout_ref[0] = state_ref[...]


def _const_spec(shape):
    return pl.BlockSpec(shape, lambda i: (0,) * len(shape), pipeline_mode=pl.Buffered(1))


def _forward_sweep(h, mod, gain, w1, rot, rd_row, wup, bup, s0):
    b, n_tok, _ = h.shape
    nb = n_tok // BLOCK
    n = b * nb
    c1 = lambda i: jnp.minimum(i, n - 1)
    c2 = lambda i: jnp.maximum(i - 1, 0)
    return pl.pallas_call(
        functools.partial(_fwd_kernel, nb=nb),
        grid=(n + 1,),
        in_specs=[
            pl.BlockSpec((1, BLOCK, D_MODEL), lambda i: (c1(i), 0, 0)),
            pl.BlockSpec((1, 3, D_MODEL), lambda i: (c1(i) // nb, 0, 0)),
            _const_spec((1, D_MODEL)),
            _const_spec((D_MODEL, HALF_W)),
            pl.BlockSpec((BLOCK, DK), lambda i: (c1(i) % nb, 0)),
            pl.BlockSpec((BLOCK, DK), lambda i: (c1(i) % nb, 0)),
            pl.BlockSpec((BLOCK, DK), lambda i: (c1(i) % nb, 0)),
            _const_spec((1, QK)),
            _const_spec((RANK_PAD, QK)),
            _const_spec((1, QK)),
            _const_spec((CHUNK, 2 * CHUNK)),
            pl.BlockSpec((1, 2 * HEADS, DK, DV), lambda i: (c2(i) // nb, 0, 0, 0)),
        ],
        out_specs=[
            pl.BlockSpec((1, BLOCK, QKV_W), lambda i: (c1(i), 0, 0)),
            pl.BlockSpec((1, BLOCK, D_MODEL), lambda i: (c1(i), 0, 0)),
            pl.BlockSpec((1, BLOCK, 2 * VW), lambda i: (c2(i), 0, 0)),
            pl.BlockSpec((1, 2 * HEADS, DK, DV), lambda i: (c2(i) // nb, 0, 0, 0)),
        ],
        out_shape=[
            jax.ShapeDtypeStruct((n, BLOCK, QKV_W), BF16),
            jax.ShapeDtypeStruct((n, BLOCK, D_MODEL), BF16),
            jax.ShapeDtypeStruct((n, BLOCK, 2 * VW), F32),
            jax.ShapeDtypeStruct((b, 2 * HEADS, DK, DV), F32),
        ],
        scratch_shapes=[
            pltpu.VMEM((2 * HEADS, DK, DV), F32),
            pltpu.VMEM((2 * HALVES, CHUNK, QKV_W), BF16),
            pltpu.VMEM((2 * HALVES, CHUNK, QK), F32),
            pltpu.VMEM((2, CHUNK, QK), BF16),
            pltpu.VMEM((CHUNK, QK), F32),
        ],
        compiler_params=pltpu.CompilerParams(
            dimension_semantics=("arbitrary",), vmem_limit_bytes=VMEM_LIMIT_BYTES),
        name="forward_sweep",
    )(h.reshape(n, BLOCK, D_MODEL), mod, gain, w1, *rot, rd_row, wup, bup, _cumsum_matrix(True), s0)


def _bwd_kernel(u_ref, h2_ref, mod2_ref, w_ref, qkv_ref, of_ref, rd_ref, wup_ref, bup_ref,
                tri_ref, rng_ref, gng_ref, wbr_ref, wbg_ref, wo_ref, fng_ref, s0_ref,
                *refs, nb, need_out, final_norm):
    if need_out:
        hout_ref, sout_ref, state_ref, g_ref, qk_ref, mask_ref, gate_ref = refs
    else:
        sout_ref, state_ref, g_ref, qk_ref, mask_ref = refs
    i = pl.program_id(0)
    j2 = lax.rem(jnp.maximum(i - 1, 0), nb)

    @pl.when(i == 0)
    def _():
        _zero_slots(g_ref, HALVES)
        _store_retention_tables(False, _log_gamma_row(rd_ref), qk_ref, mask_ref)
        if need_out:
            _zero_slots(gate_ref, HALVES)

    @pl.when(j2 == 0)
    def _():
        state_ref[...] = s0_ref[0]

    def silu(x):
        return x * jax.nn.sigmoid(x)

    def half_step(half, wr, rd):
        rows = slice(half * CHUNK, (half + 1) * CHUNK)
        u = u_ref[0, rows, :]

        def store_g(val):
            g_ref[wr] = val

        low_rank, logit, cumsum = _gla_decay_units(u, w_ref, wup_ref, bup_ref, tri_ref, store_g)

        def gate(lo):
            def run():
                x = _dot(u, w_ref[:, lo:lo + SLICE_W])
                if lo < 2 * VW:
                    norm_gain = rng_ref if lo < VW else gng_ref
                    x = silu(x) * norm_gain[:, lo % VW:lo % VW + SLICE_W]
                else:
                    x = jax.nn.sigmoid(x)
                gate_ref[wr, :, lo:lo + SLICE_W] = x.astype(BF16)
            return run

        slices = [gate(lo) for lo in range(0, QKV_W, SLICE_W)] if need_out else []

        acc = {}

        def head_norm(x, center):
            if center:
                x = x - jnp.mean(x, axis=-1, keepdims=True)
            return x * lax.rsqrt(jnp.mean(x * x, axis=-1, keepdims=True) + EPS)

        outs = {}

        def sink(branch, h, o):
            outs[branch, h] = o

        def branch_proj(branch, p):
            def run():
                w_ = wbr_ref if branch == "ret" else wbg_ref
                off = 0 if branch == "ret" else VW
                ys = []
                for h in (2 * p, 2 * p + 1):
                    gs = slice(off + h * DV, off + (h + 1) * DV)
                    x = head_norm(of_ref[0, rows, gs] + outs.pop((branch, h)), branch == "ret")
                    ys.append(x.astype(BF16) * gate_ref[rd, :, gs])
                part = _dot(jnp.concatenate(ys, axis=1), w_[2 * p * DV:(2 * p + 2) * DV, :])
                acc[branch] = part if branch not in acc else acc[branch] + part
            return run

        def out_proj():
            merged = (gate_ref[rd, :, 2 * VW:3 * VW].astype(F32) * acc.pop("ret")
                      + gate_ref[rd, :, 3 * VW:QKV_W].astype(F32) * acc.pop("gla"))
            acc["out"] = _dot(merged.astype(BF16), wo_ref[...])

        units = _scan_units(
            False,
            qkv_ref[0, rows, 0:QK], qkv_ref[0, rows, QK:V_OFF], qkv_ref[0, rows, V_OFF:G_OFF],
            qkv_ref[0, rows, G_OFF:G_OFF + QK], qkv_ref[0, rows, G_OFF + QK:G_OFF + 2 * QK],
            qkv_ref[0, rows, G_OFF + 2 * QK:QKV_W],
            g_ref[rd], _log_gamma_row(rd_ref), qk_ref, mask_ref, state_ref, sink)

        if not need_out:
            _run(_scan_order([[low_rank], [logit], [cumsum]] + [[]] * 5), units)
            return

        fill = [
            [low_rank, slices[0], slices[1]],
            [logit, slices[2]],
            [cumsum, slices[3]],
            [slices[4]],
            [branch_proj("ret", 0)],
            [slices[5]],
            [branch_proj("ret", 1)],
            [branch_proj("gla", 0), slices[6], branch_proj("gla", 1), slices[7], out_proj],
        ]
        _run(_scan_order(fill), units)

        h_new = h2_ref[0, rows, :] + mod2_ref[0, 2:3, :] * acc.pop("out")
        if final_norm:
            ms = jnp.mean(h_new * h_new, axis=-1, keepdims=True)
            h_new = h_new * lax.rsqrt(ms + EPS) * fng_ref[...]
        hout_ref[0, rows, :] = h_new

    def step(wr0, rd0):
        for half in reversed(range(HALVES)):
            half_step(half, wr0 + half, rd0 + half)

    _pipelined(i, step)

    @pl.when(j2 == nb - 1)
    def _():
        sout_ref[0] = state_ref[...]


def _backward_sweep(h, u, mod, w2, qkv, o_fwd, rd_row, wup, bup, ret_gain, gla_gain,
                    w_br, w_bg, w_o, final_gain, s0, *, need_out, final_norm):
    b, n_tok, _ = h.shape
    nb = n_tok // BLOCK
    n = b * nb

    def block_of(k):
        return (k // nb) * nb + (nb - 1 - k % nb)

    k1 = lambda i: jnp.minimum(i, n - 1)
    k2 = lambda i: jnp.maximum(i - 1, 0)
    tok1 = pl.BlockSpec((1, BLOCK, D_MODEL), lambda i: (block_of(k1(i)), 0, 0))
    tok2 = pl.BlockSpec((1, BLOCK, D_MODEL), lambda i: (block_of(k2(i)), 0, 0))
    state_spec = pl.BlockSpec((1, 2 * HEADS, DK, DV), lambda i: (k2(i) // nb, 0, 0, 0))
    out_specs = [state_spec]
    out_shape = [jax.ShapeDtypeStruct((b, 2 * HEADS, DK, DV), F32)]
    scratch = [pltpu.VMEM((2 * HEADS, DK, DV), F32), pltpu.VMEM((2 * HALVES, CHUNK, QK), F32),
               pltpu.VMEM((2, CHUNK, QK), BF16), pltpu.VMEM((CHUNK, QK), F32)]
    if need_out:
        out_specs = [tok2] + out_specs
        out_shape = [jax.ShapeDtypeStruct((n, BLOCK, D_MODEL), F32)] + out_shape
        scratch = scratch + [pltpu.VMEM((2 * HALVES, CHUNK, QKV_W), BF16)]
    outs = pl.pallas_call(
        functools.partial(_bwd_kernel, nb=nb, need_out=need_out, final_norm=final_norm),
        grid=(n + 1,),
        in_specs=[
            tok1,
            tok2,
            pl.BlockSpec((1, 3, D_MODEL), lambda i: (k2(i) // nb, 0, 0)),
            _const_spec((D_MODEL, HALF_W)),
            pl.BlockSpec((1, BLOCK, QKV_W), lambda i: (block_of(k2(i)), 0, 0)),
            pl.BlockSpec((1, BLOCK, 2 * VW), lambda i: (block_of(k2(i)), 0, 0)),
            _const_spec((1, QK)),
            _const_spec((RANK_PAD, QK)),
            _const_spec((1, QK)),
            _const_spec((CHUNK, 2 * CHUNK)),
            _const_spec((1, VW)),
            _const_spec((1, VW)),
            _const_spec((VW, D_MODEL)),
            _const_spec((VW, D_MODEL)),
            _const_spec((D_MODEL, D_MODEL)),
            _const_spec((1, D_MODEL)),
            state_spec,
        ],
        out_specs=out_specs,
        out_shape=out_shape,
        scratch_shapes=scratch,
        compiler_params=pltpu.CompilerParams(
            dimension_semantics=("arbitrary",), vmem_limit_bytes=VMEM_LIMIT_BYTES),
        name="backward_sweep",
    )(u, h.reshape(n, BLOCK, D_MODEL), mod, w2, qkv, o_fwd, rd_row, wup, bup, _cumsum_matrix(False),
      ret_gain, gla_gain, w_br, w_bg, w_o, final_gain, s0)
    if need_out:
        return outs[0].reshape(b, n_tok, D_MODEL), outs[1]
    return None, outs[0]


def _rotary_tables(n_rows):
    r_idx, c_idx = jnp.meshgrid(jnp.arange(n_rows), jnp.arange(GRID_W), indexing="ij")
    r_idx = r_idx.reshape(-1).astype(F32)
    c_idx = c_idx.reshape(-1).astype(F32)
    n_freq = DK // 4
    inv_freq = ROPE_BASE ** (-jnp.arange(n_freq, dtype=F32) / n_freq)
    ang_r = r_idx[:, None] * inv_freq
    ang_c = c_idx[:, None] * inv_freq
    ang = jnp.stack([ang_r, ang_r, ang_c, ang_c], axis=1).reshape(-1, DK)
    cos, sin = jnp.cos(ang), jnp.sin(ang)
    first = (jnp.arange(DK) // n_freq) % 2 == 0
    return cos, jnp.where(first, -sin, 0.0), jnp.where(first, 0.0, sin)


def _split_w_in(w):
    idx = [0]
    for width in (QK, QK, VW, VW, QK, QK, VW, VW, RANK, D_MODEL, D_MODEL):
        idx.append(idx[-1] + width)
    blk = [w[:, idx[i]:idx[i + 1]] for i in range(11)]
    rq, rk, rv, rg, gq, gk, gv, gg, glr, ma, mb = blk
    glr = jnp.pad(glr, ((0, 0), (0, RANK_PAD - RANK)))
    w1 = jnp.concatenate([rq, rk, rv, gq, gk, gv, glr], axis=1).astype(BF16)
    w2 = jnp.concatenate([rg, gg, ma, mb, glr], axis=1).astype(BF16)
    return w1, w2


def kernel(x, c, ctx, c_ctx, norm_gain, w_ada, b_ada, w_in, ret_decay, gla_w_up, gla_b_up, ret_norm_gain,
           gla_norm_gain, w_branch_ret, w_branch_gla, w_out, final_norm_gain):
    batch, n_lat, _ = x.shape
    n_ctx = ctx.shape[1]
    depth = w_in.shape[0]

    cvecs = jnp.concatenate([c, c_ctx[None, :], jnp.zeros((16 - batch - 1, D_MODEL), F32)], axis=0)
    mods = _modulation(cvecs, w_ada, b_ada)

    rot_lat = _rotary_tables(n_lat // GRID_W)
    rot_ctx = (jnp.ones((n_ctx, DK), F32), jnp.zeros((n_ctx, DK), F32), jnp.zeros((n_ctx, DK), F32))
    zero_state = jnp.zeros((batch, 2 * HEADS, DK, DV), F32)
    fgain = final_norm_gain.reshape(1, D_MODEL)

    h_lat, h_ctx = x, ctx
    for l in range(depth):
        last = l == depth - 1
        mod_lat = mods[l, :batch].reshape(batch, 3, D_MODEL)
        mod_ctx = jnp.broadcast_to(mods[l, batch].reshape(1, 3, D_MODEL), (batch, 3, D_MODEL))
        gain = norm_gain[l].reshape(1, D_MODEL)
        w1, w2 = _split_w_in(w_in[l])
        rd_f = jnp.repeat(ret_decay[l, 0], DK).reshape(1, QK)
        rd_b = jnp.repeat(ret_decay[l, 1], DK).reshape(1, QK)
        wup = jnp.pad(gla_w_up[l], ((0, 0), (0, RANK_PAD - RANK), (0, 0))).astype(BF16)
        bup = gla_b_up[l].reshape(2, 1, QK)
        tail = (rd_b, wup[1], bup[1], ret_norm_gain[l].reshape(1, VW), gla_norm_gain[l].reshape(1, VW),
                w_branch_ret[l].astype(BF16), w_branch_gla[l].astype(BF16), w_out[l].astype(BF16), fgain)

        qkv_c, u_c, of_c, sf_c = _forward_sweep(h_ctx, mod_ctx, gain, w1, rot_ctx, rd_f, wup[0], bup[0],
                                                zero_state)
        qkv_l, u_l, of_l, _ = _forward_sweep(h_lat, mod_lat, gain, w1, rot_lat, rd_f, wup[0], bup[0], sf_c)
        h_ctx_new, sb_c = _backward_sweep(h_ctx, u_c, mod_ctx, w2, qkv_c, of_c, *tail, zero_state,
                                          need_out=not last, final_norm=False)
        h_lat, _ = _backward_sweep(h_lat, u_l, mod_lat, w2, qkv_l, of_l, *tail, sb_c,
                                   need_out=True, final_norm=last)
        h_ctx = h_ctx_new
    return h_lat
```

```python
import functools

import jax
import jax.numpy as jnp
from jax import lax
from jax.experimental import pallas as pl
from jax.experimental.pallas import tpu as pltpu

D_MODEL = 1024
HEADS = 4
DK = 128
DV = 256
QK = HEADS * DK
VW = HEADS * DV
RANK = 16
RANK_PAD = 128
CHUNK = 128
HALVES = 2
BLOCK = CHUNK * HALVES
GLA_CHUNK = 64
LOGIT_NORM = 16.0
GRID_W = 64
ROPE_BASE = 10000.0
EPS = 1e-6

QKV_W = 2 * (2 * QK + VW)
HALF_W = QKV_W + RANK_PAD
SLICE_W = 512
V_OFF = 2 * QK
G_OFF = 2 * QK + VW
VMEM_LIMIT_BYTES = 56 * 1024 * 1024

F32 = jnp.float32
BF16 = jnp.bfloat16


def _dot(a, b):
    return jnp.dot(a, b, preferred_element_type=F32)


def _dot_nt(a, b):
    return lax.dot_general(a, b, (((1,), (1,)), ((), ())), preferred_element_type=F32)


def _dot_tn(a, b):
    return lax.dot_general(a, b, (((0,), (0,)), ((), ())), preferred_element_type=F32)


def _mod_kernel(c_ref, w_ref, b_ref, o_ref):
    c = c_ref[...]
    s = c * jax.nn.sigmoid(c)
    o_ref[0] = jnp.dot(s, w_ref[0], preferred_element_type=F32,
                       precision=lax.Precision.HIGHEST) + b_ref[0]


def _modulation(cvecs, w_ada, b_ada):
    depth = w_ada.shape[0]
    n = cvecs.shape[0]
    return pl.pallas_call(
        _mod_kernel,
        grid=(depth, 3),
        in_specs=[
            pl.BlockSpec((n, D_MODEL), lambda l, j: (0, 0)),
            pl.BlockSpec((1, D_MODEL, D_MODEL), lambda l, j: (l, 0, j)),
            pl.BlockSpec((1, 1, D_MODEL), lambda l, j: (l, 0, j)),
        ],
        out_specs=pl.BlockSpec((1, n, D_MODEL), lambda l, j: (l, 0, j)),
        out_shape=jax.ShapeDtypeStruct((depth, n, 3 * D_MODEL), F32),
        name="adaln_modulation",
    )(cvecs, w_ada, b_ada.reshape(depth, 1, 3 * D_MODEL))


def _modulated_input(h, mod_ref, gain_ref):
    ms = jnp.mean(h * h, axis=-1, keepdims=True)
    row_gain = gain_ref[...] * (1.0 + mod_ref[0, 1:2, :])
    u = h * lax.rsqrt(ms + EPS) * row_gain + mod_ref[0, 0:1, :]
    return u.astype(BF16)


def _log_sigmoid(x):
    return jnp.minimum(x, 0.0) - jnp.log(1.0 + jnp.exp(-jnp.abs(x)))


def _gla_decay_units(u, w_ref, wup_ref, bup_ref, tri_ref, store):
    hold = {}

    def low_rank():
        hold["glr"] = _dot(u, w_ref[:, QKV_W:HALF_W]).astype(BF16)

    def logit():
        x = _dot(hold.pop("glr"), wup_ref[...]) + bup_ref[...]
        log_a = _log_sigmoid(x) * (1.0 / LOGIT_NORM)
        hi = log_a.astype(BF16)
        lo = (log_a - hi.astype(F32)).astype(BF16)
        hold["hilo"] = jnp.concatenate([hi, lo], axis=0)

    def cumsum():
        store(_dot(tri_ref[...], hold.pop("hilo")))

    return low_rank, logit, cumsum


def _cumsum_matrix(fwd):
    r = jnp.arange(CHUNK)[:, None]
    c = jnp.arange(CHUNK)[None, :]
    same = (r // GLA_CHUNK) == (c // GLA_CHUNK)
    tri = jnp.where(same & ((c <= r) if fwd else (c >= r)), 1.0, 0.0).astype(BF16)
    return jnp.concatenate([tri, tri], axis=1)


def _store_retention_tables(fwd, lg_row, qk_ref, mask_ref):
    C = CHUNK
    row = lax.broadcasted_iota(jnp.int32, (C, QK), 0)
    col = lax.broadcasted_iota(jnp.int32, (C, QK), 1) & (C - 1)
    diff = (row - col) if fwd else (col - row)
    keep = (diff >= 0) if fwd else (diff > 0)
    mask_ref[...] = jnp.where(keep, jnp.exp(lg_row * jnp.where(keep, diff, 0).astype(F32)), 0.0)
    posf = row.astype(F32)
    if fwd:
        qexp, kexp = posf + 1.0, (C - 1.0) - posf
    else:
        qexp, kexp = C - posf, posf
    qk_ref[0] = jnp.exp(lg_row * qexp).astype(BF16)
    qk_ref[1] = jnp.exp(lg_row * kexp).astype(BF16)


def _scan_units(fwd, rq, rk, rv, gq, gk, gv, g, lg_row, qk_ref, mask_ref, state_ref, sink):
    C = CHUNK
    row = lax.broadcasted_iota(jnp.int32, (C, 2 * C), 0)
    col = lax.broadcasted_iota(jnp.int32, (C, 2 * C), 1) & (C - 1)
    q_dec = rq * qk_ref[0]
    k_dec = rk * qk_ref[1]
    gam = jnp.exp(lg_row * float(C))

    G = GLA_CHUNK
    ref_i, last_i = (G // 2 - 1, G - 1) if fwd else (G // 2, 0)

    def rows_of(i):
        top = jnp.broadcast_to(g[i:i + 1, :], (G, QK))
        bot = jnp.broadcast_to(g[G + i:G + i + 1, :], (G, QK))
        return jnp.concatenate([top, bot], axis=0)

    g_ref_rows = rows_of(ref_i)
    g_last_rows = rows_of(last_i)
    last_top = g[last_i:last_i + 1, :]
    last_bot = g[G + last_i:G + last_i + 1, :]
    last_p, last_q = (last_top, last_bot) if fwd else (last_bot, last_top)
    is_top = lax.broadcasted_iota(jnp.int32, (C, QK), 0) < G
    in_p = is_top if fwd else jnp.logical_not(is_top)
    q_rel = gq * jnp.exp(g - g_ref_rows).astype(BF16)
    k_rel = gk * jnp.exp(g_ref_rows - g).astype(BF16)
    q_loc = gq * jnp.exp(g).astype(BF16)
    k_end = jnp.exp(g_last_rows - g)
    k_upd = gk * (k_end * jnp.where(in_p, jnp.exp(last_q), 1.0)).astype(BF16)
    k_st = gk * jnp.where(in_p, k_end, 0.0).astype(BF16)
    q_in = gq * jnp.exp(g + jnp.where(in_p, 0.0, last_p)).astype(BF16)
    d_tot = jnp.exp(last_p + last_q)
    same = (row >> 6) == (col >> 6)
    keep_g = same & ((col <= row) if fwd else (col > row))
    q_rows = slice(G, C) if fwd else slice(0, G)

    units, hold = {}, {}

    def pair_scores(q, k, p):
        a = k[:, 2 * p * DK:(2 * p + 1) * DK]
        b = k[:, (2 * p + 1) * DK:(2 * p + 2) * DK]
        z = jnp.zeros_like(a)
        k_bd = jnp.concatenate([jnp.concatenate([a, z], axis=1), jnp.concatenate([z, b], axis=1)], axis=0)
        return _dot_nt(q[:, 2 * p * DK:(2 * p + 2) * DK], k_bd)

    def add_pair(p):
        lanes = slice(2 * p * DK, (2 * p + 2) * DK)

        def ret_s():
            s = (pair_scores(rq, rk, p) * mask_ref[:, lanes]).astype(BF16)
            hold["rs", 2 * p], hold["rs", 2 * p + 1] = s[:, :C], s[:, C:]

        def gla_s():
            local = pair_scores(q_rel, k_rel, p)
            xq = pair_scores(q_loc[q_rows, :], k_st, p)
            zeros_half = jnp.zeros((G, 2 * C), F32)
            xfull = jnp.concatenate([zeros_half, xq] if fwd else [xq, zeros_half], axis=0)
            s = jnp.where(keep_g, local, xfull).astype(BF16)
            hold["gs", 2 * p], hold["gs", 2 * p + 1] = s[:, :C], s[:, C:]

        units.update({("ret", "s", p): ret_s, ("gla", "s", p): gla_s})

    def add(h):
        sl = slice(h * DK, (h + 1) * DK)
        vs = slice(h * DV, (h + 1) * DV)

        def ret_kv():
            hold["rkv", h] = _dot_tn(k_dec[:, sl], rv[:, vs])

        def ret_o():
            s_prev = state_ref[h]
            lhs = jnp.concatenate([hold.pop(("rs", h)), q_dec[:, sl]], axis=1)
            o = _dot(lhs, jnp.concatenate([rv[:, vs], s_prev.astype(BF16)], axis=0))
            state_ref[h] = jnp.concatenate([gam[:, sl], gam[:, sl]], axis=1) * s_prev + hold.pop(("rkv", h))
            sink("ret", h, o)

        def gla_kv():
            hold["gkv", h] = _dot_tn(k_upd[:, sl], gv[:, vs])

        def gla_o():
            s_prev = state_ref[HEADS + h]
            lhs = jnp.concatenate([hold.pop(("gs", h)), q_in[:, sl]], axis=1)
            o = _dot(lhs, jnp.concatenate([gv[:, vs], s_prev.astype(BF16)], axis=0))
            dcol = jnp.broadcast_to(d_tot[:, sl], (DK, DK)).T
            state_ref[HEADS + h] = jnp.concatenate([dcol, dcol], axis=1) * s_prev + hold.pop(("gkv", h))
            sink("gla", h, o)

        units.update({("ret", "kv", h): ret_kv, ("ret", "o", h): ret_o,
                      ("gla", "kv", h): gla_kv, ("gla", "o", h): gla_o})

    for p in range(HEADS // 2):
        add_pair(p)
    for h in range(HEADS):
        add(h)
    return units


def _log_gamma_row(rd_ref):
    return jnp.log1p(-jnp.exp(rd_ref[...]))


def _pipelined(i, step):
    parity = lax.rem(i, 2)

    @pl.when(parity == 0)
    def _():
        step(0, HALVES)

    @pl.when(parity == 1)
    def _():
        step(HALVES, 0)


def _zero_slots(ref, first):
    for s in range(first, first + HALVES):
        ref[s] = jnp.zeros(ref.shape[1:], ref.dtype)


def _scan_order(fill):
    r, g = "ret", "gla"
    f = [list(x) for x in fill]
    return ([(r, "s", 0), (r, "s", 1)] + f[0]
            + [(r, "kv", 0), (r, "kv", 1), (r, "o", 0), (g, "s", 0)] + f[1]
            + [(r, "o", 1), (r, "kv", 2), (r, "kv", 3)] + f[2]
            + [(r, "o", 2), (r, "o", 3), (g, "kv", 0), (g, "kv", 1), (g, "s", 1)] + f[3]
            + [(g, "o", 0)] + f[4]
            + [(g, "o", 1), (g, "kv", 2), (g, "kv", 3)] + f[5]
            + [(g, "o", 2)] + f[6] + [(g, "o", 3)] + f[7])


def _run(order, units):
    for item in order:
        if callable(item):
            item()
        else:
            units[item]()


def _fwd_kernel(h_ref, mod_ref, gain_ref, w_ref, cos_ref, sina_ref, sinb_ref, rd_ref, wup_ref, bup_ref,
                tri_ref, s0_ref, qkv_ref, u_ref, of_ref, sout_ref, state_ref, stage_ref, g_ref, qk_ref, mask_ref,
                *, nb):
    i = pl.program_id(0)
    j2 = lax.rem(jnp.maximum(i - 1, 0), nb)

    @pl.when(i == 0)
    def _():
        _zero_slots(stage_ref, HALVES)
        _zero_slots(g_ref, HALVES)
        _store_retention_tables(True, _log_gamma_row(rd_ref), qk_ref, mask_ref)

    @pl.when(j2 == 0)
    def _():
        state_ref[...] = s0_ref[0]

    def half_step(half, wr, rd):
        rows = slice(half * CHUNK, (half + 1) * CHUNK)
        u = _modulated_input(h_ref[0, rows, :], mod_ref, gain_ref)
        u_ref[0, rows, :] = u
        cos, sina, sinb = cos_ref[rows, :], sina_ref[rows, :], sinb_ref[rows, :]

        def rotary(t):
            parts = []
            for h in range(HEADS):
                th = t[:, h * DK:(h + 1) * DK]
                parts.append(th * cos + pltpu.roll(th, 96, 1) * sina + pltpu.roll(th, 32, 1) * sinb)
            return jnp.concatenate(parts, axis=1)

        def proj(lo, post):
            def run():
                val = post(_dot(u, w_ref[:, lo:lo + SLICE_W])).astype(BF16)
                qkv_ref[0, rows, lo:lo + SLICE_W] = val
                stage_ref[wr, :, lo:lo + SLICE_W] = val
            return run

        scale = DK ** -0.5
        post = {0: lambda t: rotary(t * scale), QK: rotary, G_OFF: lambda t: t * scale}
        slices = [proj(lo, post.get(lo, lambda t: t)) for lo in range(0, QKV_W, SLICE_W)]

        def store_g(val):
            g_ref[wr] = val

        low_rank, logit, cumsum = _gla_decay_units(u, w_ref, wup_ref, bup_ref, tri_ref, store_g)

        def sink(branch, h, o):
            off = (0 if branch == "ret" else VW) + h * DV
            of_ref[0, rows, off:off + DV] = o

        units = _scan_units(
            True,
            stage_ref[rd, :, 0:QK], stage_ref[rd, :, QK:V_OFF], stage_ref[rd, :, V_OFF:G_OFF],
            stage_ref[rd, :, G_OFF:G_OFF + QK], stage_ref[rd, :, G_OFF + QK:G_OFF + 2 * QK],
            stage_ref[rd, :, G_OFF + 2 * QK:QKV_W],
            g_ref[rd], _log_gamma_row(rd_ref), qk_ref, mask_ref, state_ref, sink)

        rq_s, rk_s, rv0, rv1, gq_s, gk_s, gv0, gv1 = slices
        fill = [[low_rank, rv0], [logit, rv1], [cumsum, gq_s], [rq_s], [gk_s], [rk_s], [gv0], [gv1]]
        _run(_scan_order(fill), units)

    def step(wr0, rd0):
        for half in range(HALVES):
            half_step(half, wr0 + half, rd0 + half)

    _pipelined(i, step)

    @pl.when(j2 == nb - 1)
    def _():
        sout_ref[0] = state_ref[...]


def _const_spec(shape):
    return pl.BlockSpec(shape, lambda i: (0,) * len(shape), pipeline_mode=pl.Buffered(1))


def _forward_sweep(h, mod, gain, w1, rot, rd_row, wup, bup, s0):
    b, n_tok, _ = h.shape
    nb = n_tok // BLOCK
    n = b * nb
    c1 = lambda i: jnp.minimum(i, n - 1)
    c2 = lambda i: jnp.maximum(i - 1, 0)
    return pl.pallas_call(
        functools.partial(_fwd_kernel, nb=nb),
        grid=(n + 1,),
        in_specs=[
            pl.BlockSpec((1, BLOCK, D_MODEL), lambda i: (c1(i), 0, 0)),
            pl.BlockSpec((1, 3, D_MODEL), lambda i: (c1(i) // nb, 0, 0)),
            _const_spec((1, D_MODEL)),
            _const_spec((D_MODEL, HALF_W)),
            pl.BlockSpec((BLOCK, DK), lambda i: (c1(i) % nb, 0)),
            pl.BlockSpec((BLOCK, DK), lambda i: (c1(i) % nb, 0)),
            pl.BlockSpec((BLOCK, DK), lambda i: (c1(i) % nb, 0)),
            _const_spec((1, QK)),
            _const_spec((RANK_PAD, QK)),
            _const_spec((1, QK)),
            _const_spec((CHUNK, 2 * CHUNK)),
            pl.BlockSpec((1, 2 * HEADS, DK, DV), lambda i: (c2(i) // nb, 0, 0, 0)),
        ],
        out_specs=[
            pl.BlockSpec((1, BLOCK, QKV_W), lambda i: (c1(i), 0, 0)),
            pl.BlockSpec((1, BLOCK, D_MODEL), lambda i: (c1(i), 0, 0)),
            pl.BlockSpec((1, BLOCK, 2 * VW), lambda i: (c2(i), 0, 0)),
            pl.BlockSpec((1, 2 * HEADS, DK, DV), lambda i: (c2(i) // nb, 0, 0, 0)),
        ],
        out_shape=[
            jax.ShapeDtypeStruct((n, BLOCK, QKV_W), BF16),
            jax.ShapeDtypeStruct((n, BLOCK, D_MODEL), BF16),
            jax.ShapeDtypeStruct((n, BLOCK, 2 * VW), F32),
            jax.ShapeDtypeStruct((b, 2 * HEADS, DK, DV), F32),
        ],
        scratch_shapes=[
            pltpu.VMEM((2 * HEADS, DK, DV), F32),
            pltpu.VMEM((2 * HALVES, CHUNK, QKV_W), BF16),
            pltpu.VMEM((2 * HALVES, CHUNK, QK), F32),
            pltpu.VMEM((2, CHUNK, QK), BF16),
            pltpu.VMEM((CHUNK, QK), F32),
        ],
        compiler_params=pltpu.CompilerParams(
            dimension_semantics=("arbitrary",), vmem_limit_bytes=VMEM_LIMIT_BYTES),
        name="forward_sweep",
    )(h.reshape(n, BLOCK, D_MODEL), mod, gain, w1, *rot, rd_row, wup, bup, _cumsum_matrix(True), s0)


def _bwd_kernel(u_ref, h2_ref, mod2_ref, w_ref, qkv_ref, of_ref, rd_ref, wup_ref, bup_ref,
                tri_ref, rng_ref, gng_ref, wbr_ref, wbg_ref, wo_ref, fng_ref, s0_ref,
                *refs, nb, need_out, final_norm):
    if need_out:
        hout_ref, sout_ref, state_ref, g_ref, qk_ref, mask_ref, gate_ref = refs
    else:
        sout_ref, state_ref, g_ref, qk_ref, mask_ref = refs
    i = pl.program_id(0)
    j2 = lax.rem(jnp.maximum(i - 1, 0), nb)

    @pl.when(i == 0)
    def _():
        _zero_slots(g_ref, HALVES)
        _store_retention_tables(False, _log_gamma_row(rd_ref), qk_ref, mask_ref)
        if need_out:
            _zero_slots(gate_ref, HALVES)

    @pl.when(j2 == 0)
    def _():
        state_ref[...] = s0_ref[0]

    def silu(x):
        return x * jax.nn.sigmoid(x)

    def half_step(half, wr, rd):
        rows = slice(half * CHUNK, (half + 1) * CHUNK)
        u = u_ref[0, rows, :]

        def store_g(val):
            g_ref[wr] = val

        low_rank, logit, cumsum = _gla_decay_units(u, w_ref, wup_ref, bup_ref, tri_ref, store_g)

        def gate(lo):
            def run():
                x = _dot(u, w_ref[:, lo:lo + SLICE_W])
                if lo < 2 * VW:
                    norm_gain = rng_ref if lo < VW else gng_ref
                    x = silu(x) * norm_gain[:, lo % VW:lo % VW + SLICE_W]
                else:
                    x = jax.nn.sigmoid(x)
                gate_ref[wr, :, lo:lo + SLICE_W] = x.astype(BF16)
            return run

        slices = [gate(lo) for lo in range(0, QKV_W, SLICE_W)] if need_out else []

        acc = {}

        def head_norm(x, center):
            if center:
                x = x - jnp.mean(x, axis=-1, keepdims=True)
            return x * lax.rsqrt(jnp.mean(x * x, axis=-1, keepdims=True) + EPS)

        outs = {}

        def sink(branch, h, o):
            outs[branch, h] = o

        def branch_proj(branch, p):
            def run():
                w_ = wbr_ref if branch == "ret" else wbg_ref
                off = 0 if branch == "ret" else VW
                ys = []
                for h in (2 * p, 2 * p + 1):
                    gs = slice(off + h * DV, off + (h + 1) * DV)
                    x = head_norm(of_ref[0, rows, gs] + outs.pop((branch, h)), branch == "ret")
                    ys.append(x.astype(BF16) * gate_ref[rd, :, gs])
                part = _dot(jnp.concatenate(ys, axis=1), w_[2 * p * DV:(2 * p + 2) * DV, :])
                acc[branch] = part if branch not in acc else acc[branch] + part
            return run

        def out_proj():
            merged = (gate_ref[rd, :, 2 * VW:3 * VW].astype(F32) * acc.pop("ret")
                      + gate_ref[rd, :, 3 * VW:QKV_W].astype(F32) * acc.pop("gla"))
            acc["out"] = _dot(merged.astype(BF16), wo_ref[...])

        units = _scan_units(
            False,
            qkv_ref[0, rows, 0:QK], qkv_ref[0, rows, QK:V_OFF], qkv_ref[0, rows, V_OFF:G_OFF],
            qkv_ref[0, rows, G_OFF:G_OFF + QK], qkv_ref[0, rows, G_OFF + QK:G_OFF + 2 * QK],
            qkv_ref[0, rows, G_OFF + 2 * QK:QKV_W],
            g_ref[rd], _log_gamma_row(rd_ref), qk_ref, mask_ref, state_ref, sink)

        if not need_out:
            _run(_scan_order([[low_rank], [logit], [cumsum]] + [[]] * 5), units)
            return

        fill = [
            [low_rank, slices[0]],
            [logit, slices[1], slices[2]],
            [cumsum, slices[3]],
            [slices[4]],
            [branch_proj("ret", 0)],
            [slices[5]],
            [branch_proj("ret", 1)],
            [slices[6], branch_proj("gla", 0), slices[7], branch_proj("gla", 1), out_proj],
        ]
        _run(_scan_order(fill), units)

        h_new = h2_ref[0, rows, :] + mod2_ref[0, 2:3, :] * acc.pop("out")
        if final_norm:
            ms = jnp.mean(h_new * h_new, axis=-1, keepdims=True)
            h_new = h_new * lax.rsqrt(ms + EPS) * fng_ref[...]
        hout_ref[0, rows, :] = h_new

    def step(wr0, rd0):
        for half in reversed(range(HALVES)):
            half_step(half, wr0 + half, rd0 + half)

    _pipelined(i, step)

    @pl.when(j2 == nb - 1)
    def _():
        sout_ref[0] = state_ref[...]


def _backward_sweep(h, u, mod, w2, qkv, o_fwd, rd_row, wup, bup, ret_gain, gla_gain,
                    w_br, w_bg, w_o, final_gain, s0, *, need_out, final_norm):
    b, n_tok, _ = h.shape
    nb = n_tok // BLOCK
    n = b * nb

    def block_of(k):
        return (k // nb) * nb + (nb - 1 - k % nb)

    k1 = lambda i: jnp.minimum(i, n - 1)
    k2 = lambda i: jnp.maximum(i - 1, 0)
    tok1 = pl.BlockSpec((1, BLOCK, D_MODEL), lambda i: (block_of(k1(i)), 0, 0))
    tok2 = pl.BlockSpec((1, BLOCK, D_MODEL), lambda i: (block_of(k2(i)), 0, 0))
    state_spec = pl.BlockSpec((1, 2 * HEADS, DK, DV), lambda i: (k2(i) // nb, 0, 0, 0))
    out_specs = [state_spec]
    out_shape = [jax.ShapeDtypeStruct((b, 2 * HEADS, DK, DV), F32)]
    scratch = [pltpu.VMEM((2 * HEADS, DK, DV), F32), pltpu.VMEM((2 * HALVES, CHUNK, QK), F32),
               pltpu.VMEM((2, CHUNK, QK), BF16), pltpu.VMEM((CHUNK, QK), F32)]
    if need_out:
        out_specs = [tok2] + out_specs
        out_shape = [jax.ShapeDtypeStruct((n, BLOCK, D_MODEL), F32)] + out_shape
        scratch = scratch + [pltpu.VMEM((2 * HALVES, CHUNK, QKV_W), BF16)]
    outs = pl.pallas_call(
        functools.partial(_bwd_kernel, nb=nb, need_out=need_out, final_norm=final_norm),
        grid=(n + 1,),
        in_specs=[
            tok1,
            tok2,
            pl.BlockSpec((1, 3, D_MODEL), lambda i: (k2(i) // nb, 0, 0)),
            _const_spec((D_MODEL, HALF_W)),
            pl.BlockSpec((1, BLOCK, QKV_W), lambda i: (block_of(k2(i)), 0, 0)),
            pl.BlockSpec((1, BLOCK, 2 * VW), lambda i: (block_of(k2(i)), 0, 0)),
            _const_spec((1, QK)),
            _const_spec((RANK_PAD, QK)),
            _const_spec((1, QK)),
            _const_spec((CHUNK, 2 * CHUNK)),
            _const_spec((1, VW)),
            _const_spec((1, VW)),
            _const_spec((VW, D_MODEL)),
            _const_spec((VW, D_MODEL)),
            _const_spec((D_MODEL, D_MODEL)),
            _const_spec((1, D_MODEL)),
            state_spec,
        ],
        out_specs=out_specs,
        out_shape=out_shape,
        scratch_shapes=scratch,
        compiler_params=pltpu.CompilerParams(
            dimension_semantics=("arbitrary",), vmem_limit_bytes=VMEM_LIMIT_BYTES),
        name="backward_sweep",
    )(u, h.reshape(n, BLOCK, D_MODEL), mod, w2, qkv, o_fwd, rd_row, wup, bup, _cumsum_matrix(False),
      ret_gain, gla_gain, w_br, w_bg, w_o, final_gain, s0)
    if need_out:
        return outs[0].reshape(b, n_tok, D_MODEL), outs[1]
    return None, outs[0]


def _rotary_tables(n_rows):
    r_idx, c_idx = jnp.meshgrid(jnp.arange(n_rows), jnp.arange(GRID_W), indexing="ij")
    r_idx = r_idx.reshape(-1).astype(F32)
    c_idx = c_idx.reshape(-1).astype(F32)
    n_freq = DK // 4
    inv_freq = ROPE_BASE ** (-jnp.arange(n_freq, dtype=F32) / n_freq)
    ang_r = r_idx[:, None] * inv_freq
    ang_c = c_idx[:, None] * inv_freq
    ang = jnp.stack([ang_r, ang_r, ang_c, ang_c], axis=1).reshape(-1, DK)
    cos, sin = jnp.cos(ang), jnp.sin(ang)
    first = (jnp.arange(DK) // n_freq) % 2 == 0
    return cos, jnp.where(first, -sin, 0.0), jnp.where(first, 0.0, sin)


def _split_w_in(w):
    idx = [0]
    for width in (QK, QK, VW, VW, QK, QK, VW, VW, RANK, D_MODEL, D_MODEL):
        idx.append(idx[-1] + width)
    blk = [w[:, idx[i]:idx[i + 1]] for i in range(11)]
    rq, rk, rv, rg, gq, gk, gv, gg, glr, ma, mb = blk
    glr = jnp.pad(glr, ((0, 0), (0, RANK_PAD - RANK)))
    w1 = jnp.concatenate([rq, rk, rv, gq, gk, gv, glr], axis=1).astype(BF16)
    w2 = jnp.concatenate([rg, gg, ma, mb, glr], axis=1).astype(BF16)
    return w1, w2


def kernel(x, c, ctx, c_ctx, norm_gain, w_ada, b_ada, w_in, ret_decay, gla_w_up, gla_b_up, ret_norm_gain,
           gla_norm_gain, w_branch_ret, w_branch_gla, w_out, final_norm_gain):
    batch, n_lat, _ = x.shape
    n_ctx = ctx.shape[1]
    depth = w_in.shape[0]

    cvecs = jnp.concatenate([c, c_ctx[None, :], jnp.zeros((16 - batch - 1, D_MODEL), F32)], axis=0)
    mods = _modulation(cvecs, w_ada, b_ada)

    rot_lat = _rotary_tables(n_lat // GRID_W)
    rot_ctx = (jnp.ones((n_ctx, DK), F32), jnp.zeros((n_ctx, DK), F32), jnp.zeros((n_ctx, DK), F32))
    zero_state = jnp.zeros((batch, 2 * HEADS, DK, DV), F32)
    fgain = final_norm_gain.reshape(1, D_MODEL)

    h_lat, h_ctx = x, ctx
    for l in range(depth):
        last = l == depth - 1
        mod_lat = mods[l, :batch].reshape(batch, 3, D_MODEL)
        mod_ctx = jnp.broadcast_to(mods[l, batch].reshape(1, 3, D_MODEL), (batch, 3, D_MODEL))
        gain = norm_gain[l].reshape(1, D_MODEL)
        w1, w2 = _split_w_in(w_in[l])
        rd_f = jnp.repeat(ret_decay[l, 0], DK).reshape(1, QK)
        rd_b = jnp.repeat(ret_decay[l, 1], DK).reshape(1, QK)
        wup = jnp.pad(gla_w_up[l], ((0, 0), (0, RANK_PAD - RANK), (0, 0))).astype(BF16)
        bup = gla_b_up[l].reshape(2, 1, QK)
        tail = (rd_b, wup[1], bup[1], ret_norm_gain[l].reshape(1, VW), gla_norm_gain[l].reshape(1, VW),
                w_branch_ret[l].astype(BF16), w_branch_gla[l].astype(BF16), w_out[l].astype(BF16), fgain)

        qkv_c, u_c, of_c, sf_c = _forward_sweep(h_ctx, mod_ctx, gain, w1, rot_ctx, rd_f, wup[0], bup[0],
                                                zero_state)
        qkv_l, u_l, of_l, _ = _forward_sweep(h_lat, mod_lat, gain, w1, rot_lat, rd_f, wup[0], bup[0], sf_c)
        h_ctx_new, sb_c = _backward_sweep(h_ctx, u_c, mod_ctx, w2, qkv_c, of_c, *tail, zero_state,
                                          need_out=not last, final_norm=False)
        h_lat, _ = _backward_sweep(h_lat, u_l, mod_lat, w2, qkv_l, of_l, *tail, sb_c,
                                   need_out=True, final_norm=last)
        h_ctx = h_ctx_new
    return h_lat
```

```python
import functools

import jax
import jax.numpy as jnp
from jax import lax
from jax.experimental import pallas as pl
from jax.experimental.pallas import tpu as pltpu

D_MODEL = 1024
HEADS = 4
DK = 128
DV = 256
QK = HEADS * DK
VW = HEADS * DV
RANK = 16
RANK_PAD = 128
CHUNK = 128
FWD_CHUNKS_PER_STEP = 4
BWD_CHUNKS_PER_STEP = 2
GLA_CHUNK = 64
LOGIT_NORM = 16.0
GRID_W = 64
ROPE_BASE = 10000.0
EPS = 1e-6

QKV_W = 2 * (2 * QK + VW)
HALF_W = QKV_W + RANK_PAD
SLICE_W = 512
V_OFF = 2 * QK
G_OFF = 2 * QK + VW
VMEM_LIMIT_BYTES = 56 * 1024 * 1024

F32 = jnp.float32
BF16 = jnp.bfloat16


def _dot(a, b):
    return jnp.dot(a, b, preferred_element_type=F32)


def _dot_nt(a, b):
    return lax.dot_general(a, b, (((1,), (1,)), ((), ())), preferred_element_type=F32)


def _dot_tn(a, b):
    return lax.dot_general(a, b, (((0,), (0,)), ((), ())), preferred_element_type=F32)


def _mod_kernel(c_ref, w_ref, b_ref, o_ref):
    c = c_ref[...]
    s = c * jax.nn.sigmoid(c)
    o_ref[0] = jnp.dot(s, w_ref[0], preferred_element_type=F32,
                       precision=lax.Precision.HIGHEST) + b_ref[0]


def _modulation(cvecs, w_ada, b_ada):
    depth = w_ada.shape[0]
    n = cvecs.shape[0]
    return pl.pallas_call(
        _mod_kernel,
        grid=(depth, 3),
        in_specs=[
            pl.BlockSpec((n, D_MODEL), lambda l, j: (0, 0)),
            pl.BlockSpec((1, D_MODEL, D_MODEL), lambda l, j: (l, 0, j)),
            pl.BlockSpec((1, 1, D_MODEL), lambda l, j: (l, 0, j)),
        ],
        out_specs=pl.BlockSpec((1, n, D_MODEL), lambda l, j: (l, 0, j)),
        out_shape=jax.ShapeDtypeStruct((depth, n, 3 * D_MODEL), F32),
        name="adaln_modulation",
    )(cvecs, w_ada, b_ada.reshape(depth, 1, 3 * D_MODEL))


def _modulated_input(h, mod_ref, gain_ref):
    ms = jnp.mean(h * h, axis=-1, keepdims=True)
    row_gain = gain_ref[...] * (1.0 + mod_ref[0, 1:2, :])
    u = h * lax.rsqrt(ms + EPS) * row_gain + mod_ref[0, 0:1, :]
    return u.astype(BF16)


def _log_sigmoid(x):
    return jnp.minimum(x, 0.0) - jnp.log(1.0 + jnp.exp(-jnp.abs(x)))


def _gla_decay_units(u, w_ref, wup_ref, bup_ref, tri_ref, store):
    hold = {}

    def low_rank():
        hold["glr"] = _dot(u, w_ref[:, QKV_W:HALF_W]).astype(BF16)

    def logit():
        x = _dot(hold.pop("glr"), wup_ref[...]) + bup_ref[...]
        log_a = _log_sigmoid(x) * (1.0 / LOGIT_NORM)
        hi = log_a.astype(BF16)
        lo = (log_a - hi.astype(F32)).astype(BF16)
        hold["hilo"] = jnp.concatenate([hi, lo], axis=0)

    def cumsum():
        store(_dot(tri_ref[...], hold.pop("hilo")))

    return low_rank, logit, cumsum


def _cumsum_matrix(fwd):
    r = jnp.arange(CHUNK)[:, None]
    c = jnp.arange(CHUNK)[None, :]
    same = (r // GLA_CHUNK) == (c // GLA_CHUNK)
    tri = jnp.where(same & ((c <= r) if fwd else (c >= r)), 1.0, 0.0).astype(BF16)
    return jnp.concatenate([tri, tri], axis=1)


def _store_retention_tables(fwd, lg_row, qk_ref, mask_ref):
    C = CHUNK
    row = lax.broadcasted_iota(jnp.int32, (C, QK), 0)
    col = lax.broadcasted_iota(jnp.int32, (C, QK), 1) & (C - 1)
    diff = (row - col) if fwd else (col - row)
    keep = (diff >= 0) if fwd else (diff > 0)
    mask_ref[...] = jnp.where(keep, jnp.exp(lg_row * jnp.where(keep, diff, 0).astype(F32)), 0.0)
    posf = row.astype(F32)
    if fwd:
        qexp, kexp = posf + 1.0, (C - 1.0) - posf
    else:
        qexp, kexp = C - posf, posf
    qk_ref[0] = jnp.exp(lg_row * qexp).astype(BF16)
    qk_ref[1] = jnp.exp(lg_row * kexp).astype(BF16)


def _scan_units(fwd, rq, rk, rv, gq, gk, gv, g, lg_row, qk_ref, mask_ref, state_ref, sink):
    C = CHUNK
    row = lax.broadcasted_iota(jnp.int32, (C, 2 * C), 0)
    col = lax.broadcasted_iota(jnp.int32, (C, 2 * C), 1) & (C - 1)
    q_dec = rq * qk_ref[0]
    k_dec = rk * qk_ref[1]
    gam = jnp.exp(lg_row * float(C))

    G = GLA_CHUNK
    ref_i, last_i = (G // 2 - 1, G - 1) if fwd else (G // 2, 0)

    def rows_of(i):
        top = jnp.broadcast_to(g[i:i + 1, :], (G, QK))
        bot = jnp.broadcast_to(g[G + i:G + i + 1, :], (G, QK))
        return jnp.concatenate([top, bot], axis=0)

    g_ref_rows = rows_of(ref_i)
    g_last_rows = rows_of(last_i)
    last_top = g[last_i:last_i + 1, :]
    last_bot = g[G + last_i:G + last_i + 1, :]
    last_p, last_q = (last_top, last_bot) if fwd else (last_bot, last_top)
    is_top = lax.broadcasted_iota(jnp.int32, (C, QK), 0) < G
    in_p = is_top if fwd else jnp.logical_not(is_top)
    q_rel = gq * jnp.exp(g - g_ref_rows).astype(BF16)
    k_rel = gk * jnp.exp(g_ref_rows - g).astype(BF16)
    q_loc = gq * jnp.exp(g).astype(BF16)
    k_end = jnp.exp(g_last_rows - g)
    k_upd = gk * (k_end * jnp.where(in_p, jnp.exp(last_q), 1.0)).astype(BF16)
    k_st = gk * jnp.where(in_p, k_end, 0.0).astype(BF16)
    q_in = gq * jnp.exp(g + jnp.where(in_p, 0.0, last_p)).astype(BF16)
    d_tot = jnp.exp(last_p + last_q)
    same = (row >> 6) == (col >> 6)
    keep_g = same & ((col <= row) if fwd else (col > row))
    q_rows = slice(G, C) if fwd else slice(0, G)

    units, hold = {}, {}

    def pair_scores(q, k, p):
        a = k[:, 2 * p * DK:(2 * p + 1) * DK]
        b = k[:, (2 * p + 1) * DK:(2 * p + 2) * DK]
        z = jnp.zeros_like(a)
        k_bd = jnp.concatenate([jnp.concatenate([a, z], axis=1), jnp.concatenate([z, b], axis=1)], axis=0)
        return _dot_nt(q[:, 2 * p * DK:(2 * p + 2) * DK], k_bd)

    def add_pair(p):
        lanes = slice(2 * p * DK, (2 * p + 2) * DK)

        def ret_s():
            s = (pair_scores(rq, rk, p) * mask_ref[:, lanes]).astype(BF16)
            hold["rs", 2 * p], hold["rs", 2 * p + 1] = s[:, :C], s[:, C:]

        def gla_s():
            local = pair_scores(q_rel, k_rel, p)
            xq = pair_scores(q_loc[q_rows, :], k_st, p)
            zeros_half = jnp.zeros((G, 2 * C), F32)
            xfull = jnp.concatenate([zeros_half, xq] if fwd else [xq, zeros_half], axis=0)
            s = jnp.where(keep_g, local, xfull).astype(BF16)
            hold["gs", 2 * p], hold["gs", 2 * p + 1] = s[:, :C], s[:, C:]

        units.update({("ret", "s", p): ret_s, ("gla", "s", p): gla_s})

    def add(h):
        sl = slice(h * DK, (h + 1) * DK)
        vs = slice(h * DV, (h + 1) * DV)

        def ret_kv():
            hold["rkv", h] = _dot_tn(k_dec[:, sl], rv[:, vs])

        def ret_o():
            s_prev = state_ref[h]
            lhs = jnp.concatenate([hold.pop(("rs", h)), q_dec[:, sl]], axis=1)
            o = _dot(lhs, jnp.concatenate([rv[:, vs], s_prev.astype(BF16)], axis=0))
            state_ref[h] = jnp.concatenate([gam[:, sl], gam[:, sl]], axis=1) * s_prev + hold.pop(("rkv", h))
            sink("ret", h, o)

        def gla_kv():
            hold["gkv", h] = _dot_tn(k_upd[:, sl], gv[:, vs])

        def gla_o():
            s_prev = state_ref[HEADS + h]
            lhs = jnp.concatenate([hold.pop(("gs", h)), q_in[:, sl]], axis=1)
            o = _dot(lhs, jnp.concatenate([gv[:, vs], s_prev.astype(BF16)], axis=0))
            dcol = jnp.broadcast_to(d_tot[:, sl], (DK, DK)).T
            state_ref[HEADS + h] = jnp.concatenate([dcol, dcol], axis=1) * s_prev + hold.pop(("gkv", h))
            sink("gla", h, o)

        units.update({("ret", "kv", h): ret_kv, ("ret", "o", h): ret_o,
                      ("gla", "kv", h): gla_kv, ("gla", "o", h): gla_o})

    for p in range(HEADS // 2):
        add_pair(p)
    for h in range(HEADS):
        add(h)
    return units


def _log_gamma_row(rd_ref):
    return jnp.log1p(-jnp.exp(rd_ref[...]))


def _chunks_per_step(n_tok, want):
    return want if n_tok % (want * CHUNK) == 0 and n_tok >= 4 * want * CHUNK else 2


def _pipelined(i, step, halves):
    parity = lax.rem(i, 2)

    @pl.when(parity == 0)
    def _():
        step(0, halves)

    @pl.when(parity == 1)
    def _():
        step(halves, 0)


def _zero_slots(ref, first, count):
    for s in range(first, first + count):
        ref[s] = jnp.zeros(ref.shape[1:], ref.dtype)


def _scan_order(fill):
    r, g = "ret", "gla"
    f = [list(x) for x in fill]
    return ([(r, "s", 0), (r, "s", 1)] + f[0]
            + [(r, "kv", 0), (r, "kv", 1), (r, "o", 0), (g, "s", 0)] + f[1]
            + [(r, "o", 1), (r, "kv", 2), (r, "kv", 3)] + f[2]
            + [(r, "o", 2), (r, "o", 3), (g, "kv", 0), (g, "kv", 1), (g, "s", 1)] + f[3]
            + [(g, "o", 0)] + f[4]
            + [(g, "o", 1), (g, "kv", 2), (g, "kv", 3)] + f[5]
            + [(g, "o", 2)] + f[6] + [(g, "o", 3)] + f[7])


def _run(order, units):
    for item in order:
        if callable(item):
            item()
        else:
            units[item]()


def _fwd_kernel(h_ref, mod_ref, gain_ref, w_ref, cos_ref, sina_ref, sinb_ref, rd_ref, wup_ref, bup_ref,
                tri_ref, s0_ref, qkv_ref, u_ref, of_ref, sout_ref, state_ref, stage_ref, g_ref, qk_ref, mask_ref,
                *, nb, halves):
    i = pl.program_id(0)
    j2 = lax.rem(jnp.maximum(i - 1, 0), nb)

    @pl.when(i == 0)
    def _():
        _zero_slots(stage_ref, halves, halves)
        _zero_slots(g_ref, halves, halves)
        _store_retention_tables(True, _log_gamma_row(rd_ref), qk_ref, mask_ref)

    @pl.when(j2 == 0)
    def _():
        state_ref[...] = s0_ref[0]

    def half_step(half, wr, rd):
        rows = slice(half * CHUNK, (half + 1) * CHUNK)
        u = _modulated_input(h_ref[0, rows, :], mod_ref, gain_ref)
        u_ref[0, rows, :] = u
        cos, sina, sinb = cos_ref[rows, :], sina_ref[rows, :], sinb_ref[rows, :]

        def rotary(t):
            parts = []
            for h in range(HEADS):
                th = t[:, h * DK:(h + 1) * DK]
                parts.append(th * cos + pltpu.roll(th, 96, 1) * sina + pltpu.roll(th, 32, 1) * sinb)
            return jnp.concatenate(parts, axis=1)

        def proj(lo, post):
            def run():
                val = post(_dot(u, w_ref[:, lo:lo + SLICE_W])).astype(BF16)
                qkv_ref[0, rows, lo:lo + SLICE_W] = val
                stage_ref[wr, :, lo:lo + SLICE_W] = val
            return run

        scale = DK ** -0.5
        post = {0: lambda t: rotary(t * scale), QK: rotary, G_OFF: lambda t: t * scale}
        slices = [proj(lo, post.get(lo, lambda t: t)) for lo in range(0, QKV_W, SLICE_W)]

        def store_g(val):
            g_ref[wr] = val

        low_rank, logit, cumsum = _gla_decay_units(u, w_ref, wup_ref, bup_ref, tri_ref, store_g)

        def sink(branch, h, o):
            off = (0 if branch == "ret" else VW) + h * DV
            of_ref[0, rows, off:off + DV] = o

        units = _scan_units(
            True,
            stage_ref[rd, :, 0:QK], stage_ref[rd, :, QK:V_OFF], stage_ref[rd, :, V_OFF:G_OFF],
            stage_ref[rd, :, G_OFF:G_OFF + QK], stage_ref[rd, :, G_OFF + QK:G_OFF + 2 * QK],
            stage_ref[rd, :, G_OFF + 2 * QK:QKV_W],
            g_ref[rd], _log_gamma_row(rd_ref), qk_ref, mask_ref, state_ref, sink)

        rq_s, rk_s, rv0, rv1, gq_s, gk_s, gv0, gv1 = slices
        fill = [[low_rank, rv0], [logit, rv1], [cumsum, gq_s], [rq_s], [gk_s], [rk_s], [gv0], [gv1]]
        _run(_scan_order(fill), units)

    def step(wr0, rd0):
        for half in range(halves):
            half_step(half, wr0 + half, rd0 + half)

    _pipelined(i, step, halves)

    @pl.when(j2 == nb - 1)
    def _():
        sout_ref[0] = state_ref[...]


def _const_spec(shape):
    return pl.BlockSpec(shape, lambda i: (0,) * len(shape), pipeline_mode=pl.Buffered(1))


def _forward_sweep(h, mod, gain, w1, rot, rd_row, wup, bup, s0):
    b, n_tok, _ = h.shape
    halves = _chunks_per_step(n_tok, FWD_CHUNKS_PER_STEP)
    block = halves * CHUNK
    nb = n_tok // block
    n = b * nb
    c1 = lambda i: jnp.minimum(i, n - 1)
    c2 = lambda i: jnp.maximum(i - 1, 0)
    return pl.pallas_call(
        functools.partial(_fwd_kernel, nb=nb, halves=halves),
        grid=(n + 1,),
        in_specs=[
            pl.BlockSpec((1, block, D_MODEL), lambda i: (c1(i), 0, 0)),
            pl.BlockSpec((1, 3, D_MODEL), lambda i: (c1(i) // nb, 0, 0)),
            _const_spec((1, D_MODEL)),
            _const_spec((D_MODEL, HALF_W)),
            pl.BlockSpec((block, DK), lambda i: (c1(i) % nb, 0)),
            pl.BlockSpec((block, DK), lambda i: (c1(i) % nb, 0)),
            pl.BlockSpec((block, DK), lambda i: (c1(i) % nb, 0)),
            _const_spec((1, QK)),
            _const_spec((RANK_PAD, QK)),
            _const_spec((1, QK)),
            _const_spec((CHUNK, 2 * CHUNK)),
            pl.BlockSpec((1, 2 * HEADS, DK, DV), lambda i: (c2(i) // nb, 0, 0, 0)),
        ],
        out_specs=[
            pl.BlockSpec((1, block, QKV_W), lambda i: (c1(i), 0, 0)),
            pl.BlockSpec((1, block, D_MODEL), lambda i: (c1(i), 0, 0)),
            pl.BlockSpec((1, block, 2 * VW), lambda i: (c2(i), 0, 0)),
            pl.BlockSpec((1, 2 * HEADS, DK, DV), lambda i: (c2(i) // nb, 0, 0, 0)),
        ],
        out_shape=[
            jax.ShapeDtypeStruct((n, block, QKV_W), BF16),
            jax.ShapeDtypeStruct((n, block, D_MODEL), BF16),
            jax.ShapeDtypeStruct((n, block, 2 * VW), F32),
            jax.ShapeDtypeStruct((b, 2 * HEADS, DK, DV), F32),
        ],
        scratch_shapes=[
            pltpu.VMEM((2 * HEADS, DK, DV), F32),
            pltpu.VMEM((2 * halves, CHUNK, QKV_W), BF16),
            pltpu.VMEM((2 * halves, CHUNK, QK), F32),
            pltpu.VMEM((2, CHUNK, QK), BF16),
            pltpu.VMEM((CHUNK, QK), F32),
        ],
        compiler_params=pltpu.CompilerParams(
            dimension_semantics=("arbitrary",), vmem_limit_bytes=VMEM_LIMIT_BYTES),
        name="forward_sweep",
    )(h.reshape(n, block, D_MODEL), mod, gain, w1, *rot, rd_row, wup, bup, _cumsum_matrix(True), s0)


def _bwd_kernel(u_ref, h2_ref, mod2_ref, w_ref, qkv_ref, of_ref, rd_ref, wup_ref, bup_ref,
                tri_ref, rng_ref, gng_ref, wbr_ref, wbg_ref, wo_ref, fng_ref, s0_ref,
                *refs, nb, halves, need_out, final_norm):
    if need_out:
        hout_ref, sout_ref, state_ref, g_ref, qk_ref, mask_ref, gate_ref = refs
    else:
        sout_ref, state_ref, g_ref, qk_ref, mask_ref = refs
    i = pl.program_id(0)
    j2 = lax.rem(jnp.maximum(i - 1, 0), nb)

    @pl.when(i == 0)
    def _():
        _zero_slots(g_ref, halves, halves)
        _store_retention_tables(False, _log_gamma_row(rd_ref), qk_ref, mask_ref)
        if need_out:
            _zero_slots(gate_ref, halves, halves)

    @pl.when(j2 == 0)
    def _():
        state_ref[...] = s0_ref[0]

    def silu(x):
        return x * jax.nn.sigmoid(x)

    def half_step(half, wr, rd):
        rows = slice(half * CHUNK, (half + 1) * CHUNK)
        u = u_ref[0, rows, :]

        def store_g(val):
            g_ref[wr] = val

        low_rank, logit, cumsum = _gla_decay_units(u, w_ref, wup_ref, bup_ref, tri_ref, store_g)

        def gate(lo):
            def run():
                x = _dot(u, w_ref[:, lo:lo + SLICE_W])
                if lo < 2 * VW:
                    norm_gain = rng_ref if lo < VW else gng_ref
                    x = silu(x) * norm_gain[:, lo % VW:lo % VW + SLICE_W]
                else:
                    x = jax.nn.sigmoid(x)
                gate_ref[wr, :, lo:lo + SLICE_W] = x.astype(BF16)
            return run

        slices = [gate(lo) for lo in range(0, QKV_W, SLICE_W)] if need_out else []

        acc = {}

        def head_norm(x, center):
            if center:
                x = x - jnp.mean(x, axis=-1, keepdims=True)
            return x * lax.rsqrt(jnp.mean(x * x, axis=-1, keepdims=True) + EPS)

        outs = {}

        def sink(branch, h, o):
            outs[branch, h] = o

        def branch_proj(branch, p):
            def run():
                w_ = wbr_ref if branch == "ret" else wbg_ref
                off = 0 if branch == "ret" else VW
                ys = []
                for h in (2 * p, 2 * p + 1):
                    gs = slice(off + h * DV, off + (h + 1) * DV)
                    x = head_norm(of_ref[0, rows, gs] + outs.pop((branch, h)), branch == "ret")
                    ys.append(x.astype(BF16) * gate_ref[rd, :, gs])
                part = _dot(jnp.concatenate(ys, axis=1), w_[2 * p * DV:(2 * p + 2) * DV, :])
                acc[branch] = part if branch not in acc else acc[branch] + part
            return run

        def out_proj():
            merged = (gate_ref[rd, :, 2 * VW:3 * VW].astype(F32) * acc.pop("ret")
                      + gate_ref[rd, :, 3 * VW:QKV_W].astype(F32) * acc.pop("gla"))
            acc["out"] = _dot(merged.astype(BF16), wo_ref[...])

        units = _scan_units(
            False,
            qkv_ref[0, rows, 0:QK], qkv_ref[0, rows, QK:V_OFF], qkv_ref[0, rows, V_OFF:G_OFF],
            qkv_ref[0, rows, G_OFF:G_OFF + QK], qkv_ref[0, rows, G_OFF + QK:G_OFF + 2 * QK],
            qkv_ref[0, rows, G_OFF + 2 * QK:QKV_W],
            g_ref[rd], _log_gamma_row(rd_ref), qk_ref, mask_ref, state_ref, sink)

        if not need_out:
            _run(_scan_order([[low_rank], [logit], [cumsum]] + [[]] * 5), units)
            return

        fill = [
            [low_rank, slices[0]],
            [logit, slices[1], slices[2]],
            [cumsum, slices[3]],
            [slices[4]],
            [branch_proj("ret", 0)],
            [slices[5]],
            [branch_proj("ret", 1)],
            [slices[6], branch_proj("gla", 0), slices[7], branch_proj("gla", 1), out_proj],
        ]
        _run(_scan_order(fill), units)

        h_new = h2_ref[0, rows, :] + mod2_ref[0, 2:3, :] * acc.pop("out")
        if final_norm:
            ms = jnp.mean(h_new * h_new, axis=-1, keepdims=True)
            h_new = h_new * lax.rsqrt(ms + EPS) * fng_ref[...]
        hout_ref[0, rows, :] = h_new

    def step(wr0, rd0):
        for half in reversed(range(halves)):
            half_step(half, wr0 + half, rd0 + half)

    _pipelined(i, step, halves)

    @pl.when(j2 == nb - 1)
    def _():
        sout_ref[0] = state_ref[...]


def _backward_sweep(h, u, mod, w2, qkv, o_fwd, rd_row, wup, bup, ret_gain, gla_gain,
                    w_br, w_bg, w_o, final_gain, s0, *, need_out, final_norm):
    b, n_tok, _ = h.shape
    halves = _chunks_per_step(n_tok, BWD_CHUNKS_PER_STEP)
    block = halves * CHUNK
    nb = n_tok // block
    n = b * nb

    def block_of(k):
        return (k // nb) * nb + (nb - 1 - k % nb)

    k1 = lambda i: jnp.minimum(i, n - 1)
    k2 = lambda i: jnp.maximum(i - 1, 0)
    tok1 = pl.BlockSpec((1, block, D_MODEL), lambda i: (block_of(k1(i)), 0, 0))
    tok2 = pl.BlockSpec((1, block, D_MODEL), lambda i: (block_of(k2(i)), 0, 0))
    state_spec = pl.BlockSpec((1, 2 * HEADS, DK, DV), lambda i: (k2(i) // nb, 0, 0, 0))
    out_specs = [state_spec]
    out_shape = [jax.ShapeDtypeStruct((b, 2 * HEADS, DK, DV), F32)]
    scratch = [pltpu.VMEM((2 * HEADS, DK, DV), F32), pltpu.VMEM((2 * halves, CHUNK, QK), F32),
               pltpu.VMEM((2, CHUNK, QK), BF16), pltpu.VMEM((CHUNK, QK), F32)]
    if need_out:
        out_specs = [tok2] + out_specs
        out_shape = [jax.ShapeDtypeStruct((n, block, D_MODEL), F32)] + out_shape
        scratch = scratch + [pltpu.VMEM((2 * halves, CHUNK, QKV_W), BF16)]
    outs = pl.pallas_call(
        functools.partial(_bwd_kernel, nb=nb, halves=halves, need_out=need_out, final_norm=final_norm),
        grid=(n + 1,),
        in_specs=[
            tok1,
            tok2,
            pl.BlockSpec((1, 3, D_MODEL), lambda i: (k2(i) // nb, 0, 0)),
            _const_spec((D_MODEL, HALF_W)),
            pl.BlockSpec((1, block, QKV_W), lambda i: (block_of(k2(i)), 0, 0)),
            pl.BlockSpec((1, block, 2 * VW), lambda i: (block_of(k2(i)), 0, 0)),
            _const_spec((1, QK)),
            _const_spec((RANK_PAD, QK)),
            _const_spec((1, QK)),
            _const_spec((CHUNK, 2 * CHUNK)),
            _const_spec((1, VW)),
            _const_spec((1, VW)),
            _const_spec((VW, D_MODEL)),
            _const_spec((VW, D_MODEL)),
            _const_spec((D_MODEL, D_MODEL)),
            _const_spec((1, D_MODEL)),
            state_spec,
        ],
        out_specs=out_specs,
        out_shape=out_shape,
        scratch_shapes=scratch,
        compiler_params=pltpu.CompilerParams(
            dimension_semantics=("arbitrary",), vmem_limit_bytes=VMEM_LIMIT_BYTES),
        name="backward_sweep",
    )(u.reshape(n, block, D_MODEL), h.reshape(n, block, D_MODEL), mod, w2, qkv.reshape(n, block, QKV_W),
      o_fwd.reshape(n, block, 2 * VW), rd_row, wup, bup, _cumsum_matrix(False),
      ret_gain, gla_gain, w_br, w_bg, w_o, final_gain, s0)
    if need_out:
        return outs[0].reshape(b, n_tok, D_MODEL), outs[1]
    return None, outs[0]


def _rotary_tables(n_rows):
    r_idx, c_idx = jnp.meshgrid(jnp.arange(n_rows), jnp.arange(GRID_W), indexing="ij")
    r_idx = r_idx.reshape(-1).astype(F32)
    c_idx = c_idx.reshape(-1).astype(F32)
    n_freq = DK // 4
    inv_freq = ROPE_BASE ** (-jnp.arange(n_freq, dtype=F32) / n_freq)
    ang_r = r_idx[:, None] * inv_freq
    ang_c = c_idx[:, None] * inv_freq
    ang = jnp.stack([ang_r, ang_r, ang_c, ang_c], axis=1).reshape(-1, DK)
    cos, sin = jnp.cos(ang), jnp.sin(ang)
    first = (jnp.arange(DK) // n_freq) % 2 == 0
    return cos, jnp.where(first, -sin, 0.0), jnp.where(first, 0.0, sin)


def _split_w_in(w):
    idx = [0]
    for width in (QK, QK, VW, VW, QK, QK, VW, VW, RANK, D_MODEL, D_MODEL):
        idx.append(idx[-1] + width)
    blk = [w[:, idx[i]:idx[i + 1]] for i in range(11)]
    rq, rk, rv, rg, gq, gk, gv, gg, glr, ma, mb = blk
    glr = jnp.pad(glr, ((0, 0), (0, RANK_PAD - RANK)))
    w1 = jnp.concatenate([rq, rk, rv, gq, gk, gv, glr], axis=1).astype(BF16)
    w2 = jnp.concatenate([rg, gg, ma, mb, glr], axis=1).astype(BF16)
    return w1, w2


def kernel(x, c, ctx, c_ctx, norm_gain, w_ada, b_ada, w_in, ret_decay, gla_w_up, gla_b_up, ret_norm_gain,
           gla_norm_gain, w_branch_ret, w_branch_gla, w_out, final_norm_gain):
    batch, n_lat, _ = x.shape
    n_ctx = ctx.shape[1]
    depth = w_in.shape[0]

    cvecs = jnp.concatenate([c, c_ctx[None, :], jnp.zeros((16 - batch - 1, D_MODEL), F32)], axis=0)
    mods = _modulation(cvecs, w_ada, b_ada)

    rot_lat = _rotary_tables(n_lat // GRID_W)
    rot_ctx = (jnp.ones((n_ctx, DK), F32), jnp.zeros((n_ctx, DK), F32), jnp.zeros((n_ctx, DK), F32))
    zero_state = jnp.zeros((batch, 2 * HEADS, DK, DV), F32)
    fgain = final_norm_gain.reshape(1, D_MODEL)

    h_lat, h_ctx = x, ctx
    for l in range(depth):
        last = l == depth - 1
        mod_lat = mods[l, :batch].reshape(batch, 3, D_MODEL)
        mod_ctx = jnp.broadcast_to(mods[l, batch].reshape(1, 3, D_MODEL), (batch, 3, D_MODEL))
        gain = norm_gain[l].reshape(1, D_MODEL)
        w1, w2 = _split_w_in(w_in[l])
        rd_f = jnp.repeat(ret_decay[l, 0], DK).reshape(1, QK)
        rd_b = jnp.repeat(ret_decay[l, 1], DK).reshape(1, QK)
        wup = jnp.pad(gla_w_up[l], ((0, 0), (0, RANK_PAD - RANK), (0, 0))).astype(BF16)
        bup = gla_b_up[l].reshape(2, 1, QK)
        tail = (rd_b, wup[1], bup[1], ret_norm_gain[l].reshape(1, VW), gla_norm_gain[l].reshape(1, VW),
                w_branch_ret[l].astype(BF16), w_branch_gla[l].astype(BF16), w_out[l].astype(BF16), fgain)

        qkv_c, u_c, of_c, sf_c = _forward_sweep(h_ctx, mod_ctx, gain, w1, rot_ctx, rd_f, wup[0], bup[0],
                                                zero_state)
        qkv_l, u_l, of_l, _ = _forward_sweep(h_lat, mod_lat, gain, w1, rot_lat, rd_f, wup[0], bup[0], sf_c)
        h_ctx_new, sb_c = _backward_sweep(h_ctx, u_c, mod_ctx, w2, qkv_c, of_c, *tail, zero_state,
                                          need_out=not last, final_norm=False)
        h_lat, _ = _backward_sweep(h_lat, u_l, mod_lat, w2, qkv_l, of_l, *tail, sb_c,
                                   need_out=True, final_norm=last)
        h_ctx = h_ctx_new
    return h_lat
```

```python
import functools

import jax
import jax.numpy as jnp
from jax import lax
from jax.experimental import pallas as pl
from jax.experimental.pallas import tpu as pltpu

D_MODEL = 1024
HEADS = 4
DK = 128
DV = 256
QK = HEADS * DK
VW = HEADS * DV
RANK = 16
RANK_PAD = 128
CHUNK = 128
FWD_CHUNKS_PER_STEP = 4
BWD_CHUNKS_PER_STEP = 4
GLA_CHUNK = 64
LOGIT_NORM = 16.0
GRID_W = 64
ROPE_BASE = 10000.0
EPS = 1e-6

QKV_W = 2 * (2 * QK + VW)
HALF_W = QKV_W + RANK_PAD
SLICE_W = 512
V_OFF = 2 * QK
G_OFF = 2 * QK + VW
VMEM_CAPACITY_V7X = 64 * 1024 * 1024
VMEM_LIMIT_BYTES = VMEM_CAPACITY_V7X - 3 * 1024 * 1024

F32 = jnp.float32
BF16 = jnp.bfloat16


def _dot(a, b):
    return jnp.dot(a, b, preferred_element_type=F32)


def _dot_nt(a, b):
    return lax.dot_general(a, b, (((1,), (1,)), ((), ())), preferred_element_type=F32)


def _dot_tn(a, b):
    return lax.dot_general(a, b, (((0,), (0,)), ((), ())), preferred_element_type=F32)


def _mod_kernel(c_ref, w_ref, b_ref, o_ref):
    c = c_ref[...]
    s = c * jax.nn.sigmoid(c)
    o_ref[0] = jnp.dot(s, w_ref[0], preferred_element_type=F32,
                       precision=lax.Precision.HIGHEST) + b_ref[0]


def _modulation(cvecs, w_ada, b_ada):
    depth = w_ada.shape[0]
    n = cvecs.shape[0]
    return pl.pallas_call(
        _mod_kernel,
        grid=(depth, 3),
        in_specs=[
            pl.BlockSpec((n, D_MODEL), lambda l, j: (0, 0)),
            pl.BlockSpec((1, D_MODEL, D_MODEL), lambda l, j: (l, 0, j)),
            pl.BlockSpec((1, 1, D_MODEL), lambda l, j: (l, 0, j)),
        ],
        out_specs=pl.BlockSpec((1, n, D_MODEL), lambda l, j: (l, 0, j)),
        out_shape=jax.ShapeDtypeStruct((depth, n, 3 * D_MODEL), F32),
        name="adaln_modulation",
    )(cvecs, w_ada, b_ada.reshape(depth, 1, 3 * D_MODEL))


def _modulated_input(h, mod_ref, gain_ref):
    ms = jnp.mean(h * h, axis=-1, keepdims=True)
    row_gain = gain_ref[...] * (1.0 + mod_ref[0, 1:2, :])
    u = h * lax.rsqrt(ms + EPS) * row_gain + mod_ref[0, 0:1, :]
    return u.astype(BF16)


def _log_sigmoid(x):
    return jnp.minimum(x, 0.0) - jnp.log(1.0 + jnp.exp(-jnp.abs(x)))


def _gla_decay_units(u, w_ref, wup_ref, bup_ref, tri_ref, store):
    hold = {}

    def low_rank():
        hold["glr"] = _dot(u, w_ref[:, QKV_W:HALF_W]).astype(BF16)

    def logit():
        x = _dot(hold.pop("glr"), wup_ref[...]) + bup_ref[...]
        log_a = _log_sigmoid(x) * (1.0 / LOGIT_NORM)
        hi = log_a.astype(BF16)
        lo = (log_a - hi.astype(F32)).astype(BF16)
        hold["hilo"] = jnp.concatenate([hi, lo], axis=0)

    def cumsum():
        store(_dot(tri_ref[...], hold.pop("hilo")))

    return low_rank, logit, cumsum


def _cumsum_matrix(fwd):
    r = jnp.arange(CHUNK)[:, None]
    c = jnp.arange(CHUNK)[None, :]
    same = (r // GLA_CHUNK) == (c // GLA_CHUNK)
    tri = jnp.where(same & ((c <= r) if fwd else (c >= r)), 1.0, 0.0).astype(BF16)
    return jnp.concatenate([tri, tri], axis=1)


def _store_retention_tables(fwd, lg_row, qk_ref, mask_ref):
    C = CHUNK
    row = lax.broadcasted_iota(jnp.int32, (C, QK), 0)
    col = lax.broadcasted_iota(jnp.int32, (C, QK), 1) & (C - 1)
    diff = (row - col) if fwd else (col - row)
    keep = (diff >= 0) if fwd else (diff > 0)
    mask_ref[...] = jnp.where(keep, jnp.exp(lg_row * jnp.where(keep, diff, 0).astype(F32)), 0.0)
    posf = row.astype(F32)
    if fwd:
        qexp, kexp = posf + 1.0, (C - 1.0) - posf
    else:
        qexp, kexp = C - posf, posf
    qk_ref[0] = jnp.exp(lg_row * qexp).astype(BF16)
    qk_ref[1] = jnp.exp(lg_row * kexp).astype(BF16)


def _scan_units(fwd, rq, rk, rv, gq, gk, gv, g, lg_row, qk_ref, mask_ref, state_ref, sink):
    C = CHUNK
    row = lax.broadcasted_iota(jnp.int32, (C, 2 * C), 0)
    col = lax.broadcasted_iota(jnp.int32, (C, 2 * C), 1) & (C - 1)
    q_dec = rq * qk_ref[0]
    k_dec = rk * qk_ref[1]
    gam = jnp.exp(lg_row * float(C))

    G = GLA_CHUNK
    ref_i, last_i = (G // 2 - 1, G - 1) if fwd else (G // 2, 0)

    def rows_of(i):
        top = jnp.broadcast_to(g[i:i + 1, :], (G, QK))
        bot = jnp.broadcast_to(g[G + i:G + i + 1, :], (G, QK))
        return jnp.concatenate([top, bot], axis=0)

    g_ref_rows = rows_of(ref_i)
    g_last_rows = rows_of(last_i)
    last_top = g[last_i:last_i + 1, :]
    last_bot = g[G + last_i:G + last_i + 1, :]
    last_p, last_q = (last_top, last_bot) if fwd else (last_bot, last_top)
    is_top = lax.broadcasted_iota(jnp.int32, (C, QK), 0) < G
    in_p = is_top if fwd else jnp.logical_not(is_top)
    q_rel = gq * jnp.exp(g - g_ref_rows).astype(BF16)
    k_rel = gk * jnp.exp(g_ref_rows - g).astype(BF16)
    q_loc = gq * jnp.exp(g).astype(BF16)
    k_end = jnp.exp(g_last_rows - g)
    k_upd = gk * (k_end * jnp.where(in_p, jnp.exp(last_q), 1.0)).astype(BF16)
    k_st = gk * jnp.where(in_p, k_end, 0.0).astype(BF16)
    q_in = gq * jnp.exp(g + jnp.where(in_p, 0.0, last_p)).astype(BF16)
    d_tot = jnp.exp(last_p + last_q)
    same = (row >> 6) == (col >> 6)
    keep_g = same & ((col <= row) if fwd else (col > row))
    q_rows = slice(G, C) if fwd else slice(0, G)

    units, hold = {}, {}

    def pair_scores(q, k, p):
        a = k[:, 2 * p * DK:(2 * p + 1) * DK]
        b = k[:, (2 * p + 1) * DK:(2 * p + 2) * DK]
        z = jnp.zeros_like(a)
        k_bd = jnp.concatenate([jnp.concatenate([a, z], axis=1), jnp.concatenate([z, b], axis=1)], axis=0)
        return _dot_nt(q[:, 2 * p * DK:(2 * p + 2) * DK], k_bd)

    def add_pair(p):
        lanes = slice(2 * p * DK, (2 * p + 2) * DK)

        def ret_s():
            s = (pair_scores(rq, rk, p) * mask_ref[:, lanes]).astype(BF16)
            hold["rs", 2 * p], hold["rs", 2 * p + 1] = s[:, :C], s[:, C:]

        def gla_s():
            local = pair_scores(q_rel, k_rel, p)
            xq = pair_scores(q_loc[q_rows, :], k_st, p)
            zeros_half = jnp.zeros((G, 2 * C), F32)
            xfull = jnp.concatenate([zeros_half, xq] if fwd else [xq, zeros_half], axis=0)
            s = jnp.where(keep_g, local, xfull).astype(BF16)
            hold["gs", 2 * p], hold["gs", 2 * p + 1] = s[:, :C], s[:, C:]

        units.update({("ret", "s", p): ret_s, ("gla", "s", p): gla_s})

    def add(h):
        sl = slice(h * DK, (h + 1) * DK)
        vs = slice(h * DV, (h + 1) * DV)

        def ret_kv():
            hold["rkv", h] = _dot_tn(k_dec[:, sl], rv[:, vs])

        def ret_o():
            s_prev = state_ref[h]
            lhs = jnp.concatenate([hold.pop(("rs", h)), q_dec[:, sl]], axis=1)
            o = _dot(lhs, jnp.concatenate([rv[:, vs], s_prev.astype(BF16)], axis=0))
            state_ref[h] = jnp.concatenate([gam[:, sl], gam[:, sl]], axis=1) * s_prev + hold.pop(("rkv", h))
            sink("ret", h, o)

        def gla_kv():
            hold["gkv", h] = _dot_tn(k_upd[:, sl], gv[:, vs])

        def gla_o():
            s_prev = state_ref[HEADS + h]
            lhs = jnp.concatenate([hold.pop(("gs", h)), q_in[:, sl]], axis=1)
            o = _dot(lhs, jnp.concatenate([gv[:, vs], s_prev.astype(BF16)], axis=0))
            dcol = jnp.broadcast_to(d_tot[:, sl], (DK, DK)).T
            state_ref[HEADS + h] = jnp.concatenate([dcol, dcol], axis=1) * s_prev + hold.pop(("gkv", h))
            sink("gla", h, o)

        units.update({("ret", "kv", h): ret_kv, ("ret", "o", h): ret_o,
                      ("gla", "kv", h): gla_kv, ("gla", "o", h): gla_o})

    for p in range(HEADS // 2):
        add_pair(p)
    for h in range(HEADS):
        add(h)
    return units


def _log_gamma_row(rd_ref):
    return jnp.log1p(-jnp.exp(rd_ref[...]))


def _chunks_per_step(n_tok, want):
    return want if n_tok % (want * CHUNK) == 0 and n_tok >= 4 * want * CHUNK else 2


def _pipelined(i, step, halves):
    parity = lax.rem(i, 2)

    @pl.when(parity == 0)
    def _():
        step(0, halves)

    @pl.when(parity == 1)
    def _():
        step(halves, 0)


def _zero_slots(ref, first, count):
    for s in range(first, first + count):
        ref[s] = jnp.zeros(ref.shape[1:], ref.dtype)


def _scan_order(fill):
    r, g = "ret", "gla"
    f = [list(x) for x in fill]
    return ([(r, "s", 0), (r, "s", 1)] + f[0]
            + [(r, "kv", 0), (r, "kv", 1), (r, "o", 0), (g, "s", 0)] + f[1]
            + [(r, "o", 1), (r, "kv", 2), (r, "kv", 3)] + f[2]
            + [(r, "o", 2), (r, "o", 3), (g, "kv", 0), (g, "kv", 1), (g, "s", 1)] + f[3]
            + [(g, "o", 0)] + f[4]
            + [(g, "o", 1), (g, "kv", 2), (g, "kv", 3)] + f[5]
            + [(g, "o", 2)] + f[6] + [(g, "o", 3)] + f[7])


def _run(order, units):
    for item in order:
        if callable(item):
            item()
        else:
            units[item]()


def _fwd_kernel(h_ref, mod_ref, gain_ref, w_ref, cos_ref, sina_ref, sinb_ref, rd_ref, wup_ref, bup_ref,
                tri_ref, s0_ref, qkv_ref, u_ref, of_ref, sout_ref, state_ref, stage_ref, g_ref, qk_ref, mask_ref,
                *, nb, halves):
    i = pl.program_id(0)
    j2 = lax.rem(jnp.maximum(i - 1, 0), nb)

    @pl.when(i == 0)
    def _():
        _zero_slots(stage_ref, halves, halves)
        _zero_slots(g_ref, halves, halves)
        _store_retention_tables(True, _log_gamma_row(rd_ref), qk_ref, mask_ref)

    @pl.when(j2 == 0)
    def _():
        state_ref[...] = s0_ref[0]

    def half_step(half, wr, rd):
        rows = slice(half * CHUNK, (half + 1) * CHUNK)
        u = _modulated_input(h_ref[0, rows, :], mod_ref, gain_ref)
        u_ref[0, rows, :] = u
        cos, sina, sinb = cos_ref[rows, :], sina_ref[rows, :], sinb_ref[rows, :]

        def rotary(t):
            parts = []
            for h in range(HEADS):
                th = t[:, h * DK:(h + 1) * DK]
                parts.append(th * cos + pltpu.roll(th, 96, 1) * sina + pltpu.roll(th, 32, 1) * sinb)
            return jnp.concatenate(parts, axis=1)

        def proj(lo, post):
            def run():
                val = post(_dot(u, w_ref[:, lo:lo + SLICE_W])).astype(BF16)
                qkv_ref[0, rows, lo:lo + SLICE_W] = val
                stage_ref[wr, :, lo:lo + SLICE_W] = val
            return run

        scale = DK ** -0.5
        post = {0: lambda t: rotary(t * scale), QK: rotary, G_OFF: lambda t: t * scale}
        slices = [proj(lo, post.get(lo, lambda t: t)) for lo in range(0, QKV_W, SLICE_W)]

        def store_g(val):
            g_ref[wr] = val

        low_rank, logit, cumsum = _gla_decay_units(u, w_ref, wup_ref, bup_ref, tri_ref, store_g)

        def sink(branch, h, o):
            off = (0 if branch == "ret" else VW) + h * DV
            of_ref[0, rows, off:off + DV] = o

        units = _scan_units(
            True,
            stage_ref[rd, :, 0:QK], stage_ref[rd, :, QK:V_OFF], stage_ref[rd, :, V_OFF:G_OFF],
            stage_ref[rd, :, G_OFF:G_OFF + QK], stage_ref[rd, :, G_OFF + QK:G_OFF + 2 * QK],
            stage_ref[rd, :, G_OFF + 2 * QK:QKV_W],
            g_ref[rd], _log_gamma_row(rd_ref), qk_ref, mask_ref, state_ref, sink)

        rq_s, rk_s, rv0, rv1, gq_s, gk_s, gv0, gv1 = slices
        fill = [[low_rank, rv0], [logit, rv1], [cumsum, gq_s], [rq_s], [gk_s], [rk_s], [gv0], [gv1]]
        _run(_scan_order(fill), units)

    def step(wr0, rd0):
        for half in range(halves):
            half_step(half, wr0 + half, rd0 + half)

    _pipelined(i, step, halves)

    @pl.when(j2 == nb - 1)
    def _():
        sout_ref[0] = state_ref[...]


def _const_spec(shape):
    return pl.BlockSpec(shape, lambda i: (0,) * len(shape), pipeline_mode=pl.Buffered(1))


def _forward_sweep(h, mod, gain, w1, rot, rd_row, wup, bup, s0):
    b, n_tok, _ = h.shape
    halves = _chunks_per_step(n_tok, FWD_CHUNKS_PER_STEP)
    block = halves * CHUNK
    nb = n_tok // block
    n = b * nb
    c1 = lambda i: jnp.minimum(i, n - 1)
    c2 = lambda i: jnp.maximum(i - 1, 0)
    return pl.pallas_call(
        functools.partial(_fwd_kernel, nb=nb, halves=halves),
        grid=(n + 1,),
        in_specs=[
            pl.BlockSpec((1, block, D_MODEL), lambda i: (c1(i), 0, 0)),
            pl.BlockSpec((1, 3, D_MODEL), lambda i: (c1(i) // nb, 0, 0)),
            _const_spec((1, D_MODEL)),
            _const_spec((D_MODEL, HALF_W)),
            pl.BlockSpec((block, DK), lambda i: (c1(i) % nb, 0)),
            pl.BlockSpec((block, DK), lambda i: (c1(i) % nb, 0)),
            pl.BlockSpec((block, DK), lambda i: (c1(i) % nb, 0)),
            _const_spec((1, QK)),
            _const_spec((RANK_PAD, QK)),
            _const_spec((1, QK)),
            _const_spec((CHUNK, 2 * CHUNK)),
            pl.BlockSpec((1, 2 * HEADS, DK, DV), lambda i: (c2(i) // nb, 0, 0, 0)),
        ],
        out_specs=[
            pl.BlockSpec((1, block, QKV_W), lambda i: (c1(i), 0, 0)),
            pl.BlockSpec((1, block, D_MODEL), lambda i: (c1(i), 0, 0)),
            pl.BlockSpec((1, block, 2 * VW), lambda i: (c2(i), 0, 0)),
            pl.BlockSpec((1, 2 * HEADS, DK, DV), lambda i: (c2(i) // nb, 0, 0, 0)),
        ],
        out_shape=[
            jax.ShapeDtypeStruct((n, block, QKV_W), BF16),
            jax.ShapeDtypeStruct((n, block, D_MODEL), BF16),
            jax.ShapeDtypeStruct((n, block, 2 * VW), F32),
            jax.ShapeDtypeStruct((b, 2 * HEADS, DK, DV), F32),
        ],
        scratch_shapes=[
            pltpu.VMEM((2 * HEADS, DK, DV), F32),
            pltpu.VMEM((2 * halves, CHUNK, QKV_W), BF16),
            pltpu.VMEM((2 * halves, CHUNK, QK), F32),
            pltpu.VMEM((2, CHUNK, QK), BF16),
            pltpu.VMEM((CHUNK, QK), F32),
        ],
        compiler_params=pltpu.CompilerParams(
            dimension_semantics=("arbitrary",), vmem_limit_bytes=VMEM_LIMIT_BYTES),
        name="forward_sweep",
    )(h.reshape(n, block, D_MODEL), mod, gain, w1, *rot, rd_row, wup, bup, _cumsum_matrix(True), s0)


def _bwd_kernel(u_ref, h2_ref, mod2_ref, w_ref, qkv_ref, of_ref, rd_ref, wup_ref, bup_ref,
                tri_ref, rng_ref, gng_ref, wbr_ref, wbg_ref, wo_ref, fng_ref, s0_ref,
                *refs, nb, halves, need_out, final_norm):
    if need_out:
        hout_ref, sout_ref, state_ref, g_ref, qk_ref, mask_ref, gate_ref = refs
    else:
        sout_ref, state_ref, g_ref, qk_ref, mask_ref = refs
    i = pl.program_id(0)
    j2 = lax.rem(jnp.maximum(i - 1, 0), nb)

    @pl.when(i == 0)
    def _():
        _zero_slots(g_ref, halves, halves)
        _store_retention_tables(False, _log_gamma_row(rd_ref), qk_ref, mask_ref)
        if need_out:
            _zero_slots(gate_ref, halves, halves)

    @pl.when(j2 == 0)
    def _():
        state_ref[...] = s0_ref[0]

    def silu(x):
        return x * jax.nn.sigmoid(x)

    def half_step(half, wr, rd):
        rows = slice(half * CHUNK, (half + 1) * CHUNK)
        u = u_ref[0, rows, :]

        def store_g(val):
            g_ref[wr] = val

        low_rank, logit, cumsum = _gla_decay_units(u, w_ref, wup_ref, bup_ref, tri_ref, store_g)

        def gate(lo):
            def run():
                x = _dot(u, w_ref[:, lo:lo + SLICE_W])
                if lo < 2 * VW:
                    norm_gain = rng_ref if lo < VW else gng_ref
                    x = silu(x) * norm_gain[:, lo % VW:lo % VW + SLICE_W]
                else:
                    x = jax.nn.sigmoid(x)
                gate_ref[wr, :, lo:lo + SLICE_W] = x.astype(BF16)
            return run

        slices = [gate(lo) for lo in range(0, QKV_W, SLICE_W)] if need_out else []

        acc = {}

        def head_norm(x, center):
            if center:
                x = x - jnp.mean(x, axis=-1, keepdims=True)
            return x * lax.rsqrt(jnp.mean(x * x, axis=-1, keepdims=True) + EPS)

        outs = {}

        def sink(branch, h, o):
            outs[branch, h] = o

        def branch_proj(branch, p):
            def run():
                w_ = wbr_ref if branch == "ret" else wbg_ref
                off = 0 if branch == "ret" else VW
                ys = []
                for h in (2 * p, 2 * p + 1):
                    gs = slice(off + h * DV, off + (h + 1) * DV)
                    x = head_norm(of_ref[0, rows, gs] + outs.pop((branch, h)), branch == "ret")
                    ys.append(x.astype(BF16) * gate_ref[rd, :, gs])
                part = _dot(jnp.concatenate(ys, axis=1), w_[2 * p * DV:(2 * p + 2) * DV, :])
                acc[branch] = part if branch not in acc else acc[branch] + part
            return run

        def out_proj():
            merged = (gate_ref[rd, :, 2 * VW:3 * VW].astype(F32) * acc.pop("ret")
                      + gate_ref[rd, :, 3 * VW:QKV_W].astype(F32) * acc.pop("gla"))
            acc["out"] = _dot(merged.astype(BF16), wo_ref[...])

        units = _scan_units(
            False,
            qkv_ref[0, rows, 0:QK], qkv_ref[0, rows, QK:V_OFF], qkv_ref[0, rows, V_OFF:G_OFF],
            qkv_ref[0, rows, G_OFF:G_OFF + QK], qkv_ref[0, rows, G_OFF + QK:G_OFF + 2 * QK],
            qkv_ref[0, rows, G_OFF + 2 * QK:QKV_W],
            g_ref[rd], _log_gamma_row(rd_ref), qk_ref, mask_ref, state_ref, sink)

        if not need_out:
            _run(_scan_order([[low_rank], [logit], [cumsum]] + [[]] * 5), units)
            return

        fill = [
            [low_rank, slices[0]],
            [logit, slices[1], slices[2]],
            [cumsum, slices[3]],
            [slices[4]],
            [branch_proj("ret", 0)],
            [slices[5]],
            [branch_proj("ret", 1)],
            [slices[6], branch_proj("gla", 0), slices[7], branch_proj("gla", 1), out_proj],
        ]
        _run(_scan_order(fill), units)

        h_new = h2_ref[0, rows, :] + mod2_ref[0, 2:3, :] * acc.pop("out")
        if final_norm:
            ms = jnp.mean(h_new * h_new, axis=-1, keepdims=True)
            h_new = h_new * lax.rsqrt(ms + EPS) * fng_ref[...]
        hout_ref[0, rows, :] = h_new

    def step(wr0, rd0):
        for half in reversed(range(halves)):
            half_step(half, wr0 + half, rd0 + half)

    _pipelined(i, step, halves)

    @pl.when(j2 == nb - 1)
    def _():
        sout_ref[0] = state_ref[...]


def _backward_sweep(h, u, mod, w2, qkv, o_fwd, rd_row, wup, bup, ret_gain, gla_gain,
                    w_br, w_bg, w_o, final_gain, s0, *, need_out, final_norm):
    b, n_tok, _ = h.shape
    halves = _chunks_per_step(n_tok, BWD_CHUNKS_PER_STEP)
    block = halves * CHUNK
    nb = n_tok // block
    n = b * nb

    def block_of(k):
        return (k // nb) * nb + (nb - 1 - k % nb)

    k1 = lambda i: jnp.minimum(i, n - 1)
    k2 = lambda i: jnp.maximum(i - 1, 0)
    tok1 = pl.BlockSpec((1, block, D_MODEL), lambda i: (block_of(k1(i)), 0, 0))
    tok2 = pl.BlockSpec((1, block, D_MODEL), lambda i: (block_of(k2(i)), 0, 0))
    state_spec = pl.BlockSpec((1, 2 * HEADS, DK, DV), lambda i: (k2(i) // nb, 0, 0, 0))
    out_specs = [state_spec]
    out_shape = [jax.ShapeDtypeStruct((b, 2 * HEADS, DK, DV), F32)]
    scratch = [pltpu.VMEM((2 * HEADS, DK, DV), F32), pltpu.VMEM((2 * halves, CHUNK, QK), F32),
               pltpu.VMEM((2, CHUNK, QK), BF16), pltpu.VMEM((CHUNK, QK), F32)]
    if need_out:
        out_specs = [tok2] + out_specs
        out_shape = [jax.ShapeDtypeStruct((n, block, D_MODEL), F32)] + out_shape
        scratch = scratch + [pltpu.VMEM((2 * halves, CHUNK, QKV_W), BF16)]
    outs = pl.pallas_call(
        functools.partial(_bwd_kernel, nb=nb, halves=halves, need_out=need_out, final_norm=final_norm),
        grid=(n + 1,),
        in_specs=[
            tok1,
            tok2,
            pl.BlockSpec((1, 3, D_MODEL), lambda i: (k2(i) // nb, 0, 0)),
            _const_spec((D_MODEL, HALF_W)),
            pl.BlockSpec((1, block, QKV_W), lambda i: (block_of(k2(i)), 0, 0)),
            pl.BlockSpec((1, block, 2 * VW), lambda i: (block_of(k2(i)), 0, 0)),
            _const_spec((1, QK)),
            _const_spec((RANK_PAD, QK)),
            _const_spec((1, QK)),
            _const_spec((CHUNK, 2 * CHUNK)),
            _const_spec((1, VW)),
            _const_spec((1, VW)),
            _const_spec((VW, D_MODEL)),
            _const_spec((VW, D_MODEL)),
            _const_spec((D_MODEL, D_MODEL)),
            _const_spec((1, D_MODEL)),
            state_spec,
        ],
        out_specs=out_specs,
        out_shape=out_shape,
        scratch_shapes=scratch,
        compiler_params=pltpu.CompilerParams(
            dimension_semantics=("arbitrary",), vmem_limit_bytes=VMEM_LIMIT_BYTES),
        name="backward_sweep",
    )(u.reshape(n, block, D_MODEL), h.reshape(n, block, D_MODEL), mod, w2, qkv.reshape(n, block, QKV_W),
      o_fwd.reshape(n, block, 2 * VW), rd_row, wup, bup, _cumsum_matrix(False),
      ret_gain, gla_gain, w_br, w_bg, w_o, final_gain, s0)
    if need_out:
        return outs[0].reshape(b, n_tok, D_MODEL), outs[1]
    return None, outs[0]


def _rotary_tables(n_rows):
    r_idx, c_idx = jnp.meshgrid(jnp.arange(n_rows), jnp.arange(GRID_W), indexing="ij")
    r_idx = r_idx.reshape(-1).astype(F32)
    c_idx = c_idx.reshape(-1).astype(F32)
    n_freq = DK // 4
    inv_freq = ROPE_BASE ** (-jnp.arange(n_freq, dtype=F32) / n_freq)
    ang_r = r_idx[:, None] * inv_freq
    ang_c = c_idx[:, None] * inv_freq
    ang = jnp.stack([ang_r, ang_r, ang_c, ang_c], axis=1).reshape(-1, DK)
    cos, sin = jnp.cos(ang), jnp.sin(ang)
    first = (jnp.arange(DK) // n_freq) % 2 == 0
    return cos, jnp.where(first, -sin, 0.0), jnp.where(first, 0.0, sin)


def _split_w_in(w):
    idx = [0]
    for width in (QK, QK, VW, VW, QK, QK, VW, VW, RANK, D_MODEL, D_MODEL):
        idx.append(idx[-1] + width)
    blk = [w[:, idx[i]:idx[i + 1]] for i in range(11)]
    rq, rk, rv, rg, gq, gk, gv, gg, glr, ma, mb = blk
    glr = jnp.pad(glr, ((0, 0), (0, RANK_PAD - RANK)))
    w1 = jnp.concatenate([rq, rk, rv, gq, gk, gv, glr], axis=1).astype(BF16)
    w2 = jnp.concatenate([rg, gg, ma, mb, glr], axis=1).astype(BF16)
    return w1, w2


def kernel(x, c, ctx, c_ctx, norm_gain, w_ada, b_ada, w_in, ret_decay, gla_w_up, gla_b_up, ret_norm_gain,
           gla_norm_gain, w_branch_ret, w_branch_gla, w_out, final_norm_gain):
    batch, n_lat, _ = x.shape
    n_ctx = ctx.shape[1]
    depth = w_in.shape[0]

    cvecs = jnp.concatenate([c, c_ctx[None, :], jnp.zeros((16 - batch - 1, D_MODEL), F32)], axis=0)
    mods = _modulation(cvecs, w_ada, b_ada)

    rot_lat = _rotary_tables(n_lat // GRID_W)
    rot_ctx = (jnp.ones((n_ctx, DK), F32), jnp.zeros((n_ctx, DK), F32), jnp.zeros((n_ctx, DK), F32))
    zero_state = jnp.zeros((batch, 2 * HEADS, DK, DV), F32)
    fgain = final_norm_gain.reshape(1, D_MODEL)

    h_lat, h_ctx = x, ctx
    for l in range(depth):
        last = l == depth - 1
        mod_lat = mods[l, :batch].reshape(batch, 3, D_MODEL)
        mod_ctx = jnp.broadcast_to(mods[l, batch].reshape(1, 3, D_MODEL), (batch, 3, D_MODEL))
        gain = norm_gain[l].reshape(1, D_MODEL)
        w1, w2 = _split_w_in(w_in[l])
        rd_f = jnp.repeat(ret_decay[l, 0], DK).reshape(1, QK)
        rd_b = jnp.repeat(ret_decay[l, 1], DK).reshape(1, QK)
        wup = jnp.pad(gla_w_up[l], ((0, 0), (0, RANK_PAD - RANK), (0, 0))).astype(BF16)
        bup = gla_b_up[l].reshape(2, 1, QK)
        tail = (rd_b, wup[1], bup[1], ret_norm_gain[l].reshape(1, VW), gla_norm_gain[l].reshape(1, VW),
                w_branch_ret[l].astype(BF16), w_branch_gla[l].astype(BF16), w_out[l].astype(BF16), fgain)

        qkv_c, u_c, of_c, sf_c = _forward_sweep(h_ctx, mod_ctx, gain, w1, rot_ctx, rd_f, wup[0], bup[0],
                                                zero_state)
        qkv_l, u_l, of_l, _ = _forward_sweep(h_lat, mod_lat, gain, w1, rot_lat, rd_f, wup[0], bup[0], sf_c)
        h_ctx_new, sb_c = _backward_sweep(h_ctx, u_c, mod_ctx, w2, qkv_c, of_c, *tail, zero_state,
                                          need_out=not last, final_norm=False)
        h_lat, _ = _backward_sweep(h_lat, u_l, mod_lat, w2, qkv_l, of_l, *tail, sb_c,
                                   need_out=True, final_norm=last)
        h_ctx = h_ctx_new
    return h_lat
```

```python
import functools

import jax
import jax.numpy as jnp
from jax import lax
from jax.experimental import pallas as pl
from jax.experimental.pallas import tpu as pltpu

D_MODEL = 1024
HEADS = 4
DK = 128
DV = 256
QK = HEADS * DK
VW = HEADS * DV
RANK = 16
RANK_PAD = 128
CHUNK = 128
FWD_CHUNKS_PER_STEP = 4
BWD_CHUNKS_PER_STEP = 4
GLA_CHUNK = 64
LOGIT_NORM = 16.0
GRID_W = 64
ROPE_BASE = 10000.0
EPS = 1e-6

QKV_W = 2 * (2 * QK + VW)
MERGE_W = 2 * D_MODEL
FWD_W = QKV_W + MERGE_W + RANK_PAD
BWD_W = 2 * VW + RANK_PAD
SLICE_W = 512
V_OFF = 2 * QK
G_OFF = 2 * QK + VW
VMEM_CAPACITY_V7X = 64 * 1024 * 1024
VMEM_LIMIT_BYTES = VMEM_CAPACITY_V7X - 3 * 1024 * 1024

F32 = jnp.float32
BF16 = jnp.bfloat16


def _dot(a, b):
    return jnp.dot(a, b, preferred_element_type=F32)


def _dot_nt(a, b):
    return lax.dot_general(a, b, (((1,), (1,)), ((), ())), preferred_element_type=F32)


def _dot_tn(a, b):
    return lax.dot_general(a, b, (((0,), (0,)), ((), ())), preferred_element_type=F32)


def _mod_kernel(c_ref, w_ref, b_ref, o_ref):
    c = c_ref[...]
    s = c * jax.nn.sigmoid(c)
    o_ref[0] = jnp.dot(s, w_ref[0], preferred_element_type=F32,
                       precision=lax.Precision.HIGHEST) + b_ref[0]


def _modulation(cvecs, w_ada, b_ada):
    depth = w_ada.shape[0]
    n = cvecs.shape[0]
    return pl.pallas_call(
        _mod_kernel,
        grid=(depth, 3),
        in_specs=[
            pl.BlockSpec((n, D_MODEL), lambda l, j: (0, 0)),
            pl.BlockSpec((1, D_MODEL, D_MODEL), lambda l, j: (l, 0, j)),
            pl.BlockSpec((1, 1, D_MODEL), lambda l, j: (l, 0, j)),
        ],
        out_specs=pl.BlockSpec((1, n, D_MODEL), lambda l, j: (l, 0, j)),
        out_shape=jax.ShapeDtypeStruct((depth, n, 3 * D_MODEL), F32),
        name="adaln_modulation",
    )(cvecs, w_ada, b_ada.reshape(depth, 1, 3 * D_MODEL))


def _modulated_input(h, mod_ref, gain_ref):
    ms = jnp.mean(h * h, axis=-1, keepdims=True)
    row_gain = gain_ref[...] * (1.0 + mod_ref[0, 1:2, :])
    u = h * lax.rsqrt(ms + EPS) * row_gain + mod_ref[0, 0:1, :]
    return u.astype(BF16)


def _log_sigmoid(x):
    return jnp.minimum(x, 0.0) - jnp.log(1.0 + jnp.exp(-jnp.abs(x)))


def _gla_decay_units(u, w_ref, wup_ref, bup_ref, tri_ref, store):
    hold = {}

    def low_rank():
        hold["glr"] = _dot(u, w_ref[:, w_ref.shape[1] - RANK_PAD:]).astype(BF16)

    def logit():
        x = _dot(hold.pop("glr"), wup_ref[...]) + bup_ref[...]
        log_a = _log_sigmoid(x) * (1.0 / LOGIT_NORM)
        hi = log_a.astype(BF16)
        lo = (log_a - hi.astype(F32)).astype(BF16)
        hold["hilo"] = jnp.concatenate([hi, lo], axis=0)

    def cumsum():
        store(_dot(tri_ref[...], hold.pop("hilo")))

    return low_rank, logit, cumsum


def _cumsum_matrix(fwd):
    r = jnp.arange(CHUNK)[:, None]
    c = jnp.arange(CHUNK)[None, :]
    same = (r // GLA_CHUNK) == (c // GLA_CHUNK)
    tri = jnp.where(same & ((c <= r) if fwd else (c >= r)), 1.0, 0.0).astype(BF16)
    return jnp.concatenate([tri, tri], axis=1)


def _store_retention_tables(fwd, lg_row, qk_ref, mask_ref):
    C = CHUNK
    row = lax.broadcasted_iota(jnp.int32, (C, QK), 0)
    col = lax.broadcasted_iota(jnp.int32, (C, QK), 1) & (C - 1)
    diff = (row - col) if fwd else (col - row)
    keep = (diff >= 0) if fwd else (diff > 0)
    mask_ref[...] = jnp.where(keep, jnp.exp(lg_row * jnp.where(keep, diff, 0).astype(F32)), 0.0)
    posf = row.astype(F32)
    if fwd:
        qexp, kexp = posf + 1.0, (C - 1.0) - posf
    else:
        qexp, kexp = C - posf, posf
    qk_ref[0] = jnp.exp(lg_row * qexp).astype(BF16)
    qk_ref[1] = jnp.exp(lg_row * kexp).astype(BF16)


def _scan_units(fwd, rq, rk, rv, gq, gk, gv, g, lg_row, qk_ref, mask_ref, state_ref, sink):
    C = CHUNK
    row = lax.broadcasted_iota(jnp.int32, (C, 2 * C), 0)
    col = lax.broadcasted_iota(jnp.int32, (C, 2 * C), 1) & (C - 1)
    q_dec = rq * qk_ref[0]
    k_dec = rk * qk_ref[1]
    gam = jnp.exp(lg_row * float(C))

    G = GLA_CHUNK
    ref_i, last_i = (G // 2 - 1, G - 1) if fwd else (G // 2, 0)

    def rows_of(i):
        top = jnp.broadcast_to(g[i:i + 1, :], (G, QK))
        bot = jnp.broadcast_to(g[G + i:G + i + 1, :], (G, QK))
        return jnp.concatenate([top, bot], axis=0)

    g_ref_rows = rows_of(ref_i)
    g_last_rows = rows_of(last_i)
    last_top = g[last_i:last_i + 1, :]
    last_bot = g[G + last_i:G + last_i + 1, :]
    last_p, last_q = (last_top, last_bot) if fwd else (last_bot, last_top)
    is_top = lax.broadcasted_iota(jnp.int32, (C, QK), 0) < G
    in_p = is_top if fwd else jnp.logical_not(is_top)
    q_rel = gq * jnp.exp(g - g_ref_rows).astype(BF16)
    k_rel = gk * jnp.exp(g_ref_rows - g).astype(BF16)
    q_loc = gq * jnp.exp(g).astype(BF16)
    k_end = jnp.exp(g_last_rows - g)
    k_upd = gk * (k_end * jnp.where(in_p, jnp.exp(last_q), 1.0)).astype(BF16)
    k_st = gk * jnp.where(in_p, k_end, 0.0).astype(BF16)
    q_in = gq * jnp.exp(g + jnp.where(in_p, 0.0, last_p)).astype(BF16)
    d_tot = jnp.exp(last_p + last_q)
    same = (row >> 6) == (col >> 6)
    keep_g = same & ((col <= row) if fwd else (col > row))
    q_rows = slice(G, C) if fwd else slice(0, G)

    units, hold = {}, {}

    def pair_scores(q, k, p):
        a = k[:, 2 * p * DK:(2 * p + 1) * DK]
        b = k[:, (2 * p + 1) * DK:(2 * p + 2) * DK]
        z = jnp.zeros_like(a)
        k_bd = jnp.concatenate([jnp.concatenate([a, z], axis=1), jnp.concatenate([z, b], axis=1)], axis=0)
        return _dot_nt(q[:, 2 * p * DK:(2 * p + 2) * DK], k_bd)

    def add_pair(p):
        lanes = slice(2 * p * DK, (2 * p + 2) * DK)

        def ret_s():
            s = (pair_scores(rq, rk, p) * mask_ref[:, lanes]).astype(BF16)
            hold["rs", 2 * p], hold["rs", 2 * p + 1] = s[:, :C], s[:, C:]

        def gla_s():
            local = pair_scores(q_rel, k_rel, p)
            xq = pair_scores(q_loc[q_rows, :], k_st, p)
            zeros_half = jnp.zeros((G, 2 * C), F32)
            xfull = jnp.concatenate([zeros_half, xq] if fwd else [xq, zeros_half], axis=0)
            s = jnp.where(keep_g, local, xfull).astype(BF16)
            hold["gs", 2 * p], hold["gs", 2 * p + 1] = s[:, :C], s[:, C:]

        units.update({("ret", "s", p): ret_s, ("gla", "s", p): gla_s})

    def add(h):
        sl = slice(h * DK, (h + 1) * DK)
        vs = slice(h * DV, (h + 1) * DV)

        def ret_kv():
            hold["rkv", h] = _dot_tn(k_dec[:, sl], rv[:, vs])

        def ret_o():
            s_prev = state_ref[h]
            lhs = jnp.concatenate([hold.pop(("rs", h)), q_dec[:, sl]], axis=1)
            o = _dot(lhs, jnp.concatenate([rv[:, vs], s_prev.astype(BF16)], axis=0))
            state_ref[h] = jnp.concatenate([gam[:, sl], gam[:, sl]], axis=1) * s_prev + hold.pop(("rkv", h))
            sink("ret", h, o)

        def gla_kv():
            hold["gkv", h] = _dot_tn(k_upd[:, sl], gv[:, vs])

        def gla_o():
            s_prev = state_ref[HEADS + h]
            lhs = jnp.concatenate([hold.pop(("gs", h)), q_in[:, sl]], axis=1)
            o = _dot(lhs, jnp.concatenate([gv[:, vs], s_prev.astype(BF16)], axis=0))
            dcol = jnp.broadcast_to(d_tot[:, sl], (DK, DK)).T
            state_ref[HEADS + h] = jnp.concatenate([dcol, dcol], axis=1) * s_prev + hold.pop(("gkv", h))
            sink("gla", h, o)

        units.update({("ret", "kv", h): ret_kv, ("ret", "o", h): ret_o,
                      ("gla", "kv", h): gla_kv, ("gla", "o", h): gla_o})

    for p in range(HEADS // 2):
        add_pair(p)
    for h in range(HEADS):
        add(h)
    return units


def _log_gamma_row(rd_ref):
    return jnp.log1p(-jnp.exp(rd_ref[...]))


def _chunks_per_step(n_tok, want):
    return want if n_tok % (want * CHUNK) == 0 and n_tok >= 4 * want * CHUNK else 2


def _pipelined(i, step, halves):
    parity = lax.rem(i, 2)

    @pl.when(parity == 0)
    def _():
        step(0, halves)

    @pl.when(parity == 1)
    def _():
        step(halves, 0)


def _zero_slots(ref, first, count):
    for s in range(first, first + count):
        ref[s] = jnp.zeros(ref.shape[1:], ref.dtype)


def _scan_order(fill):
    r, g = "ret", "gla"
    f = [list(x) for x in fill]
    return ([(r, "s", 0), (r, "s", 1)] + f[0]
            + [(r, "kv", 0), (r, "kv", 1), (r, "o", 0), (g, "s", 0)] + f[1]
            + [(r, "o", 1), (r, "kv", 2), (r, "kv", 3)] + f[2]
            + [(r, "o", 2), (r, "o", 3), (g, "kv", 0), (g, "kv", 1), (g, "s", 1)] + f[3]
            + [(g, "o", 0)] + f[4]
            + [(g, "o", 1), (g, "kv", 2), (g, "kv", 3)] + f[5]
            + [(g, "o", 2)] + f[6] + [(g, "o", 3)] + f[7])


def _run(order, units):
    for item in order:
        if callable(item):
            item()
        else:
            units[item]()


def _fwd_kernel(h_ref, mod_ref, gain_ref, w_ref, cos_ref, sina_ref, sinb_ref, rd_ref, wup_ref, bup_ref,
                tri_ref, s0_ref, qkv_ref, u_ref, mg_ref, of_ref, sout_ref, state_ref, stage_ref, g_ref, qk_ref, mask_ref,
                *, nb, halves):
    i = pl.program_id(0)
    j2 = lax.rem(jnp.maximum(i - 1, 0), nb)

    @pl.when(i == 0)
    def _():
        _zero_slots(stage_ref, halves, halves)
        _zero_slots(g_ref, halves, halves)
        _store_retention_tables(True, _log_gamma_row(rd_ref), qk_ref, mask_ref)

    @pl.when(j2 == 0)
    def _():
        state_ref[...] = s0_ref[0]

    def half_step(half, wr, rd):
        rows = slice(half * CHUNK, (half + 1) * CHUNK)
        u = _modulated_input(h_ref[0, rows, :], mod_ref, gain_ref)
        u_ref[0, rows, :] = u
        cos, sina, sinb = cos_ref[rows, :], sina_ref[rows, :], sinb_ref[rows, :]

        def rotary(t):
            parts = []
            for h in range(HEADS):
                th = t[:, h * DK:(h + 1) * DK]
                parts.append(th * cos + pltpu.roll(th, 96, 1) * sina + pltpu.roll(th, 32, 1) * sinb)
            return jnp.concatenate(parts, axis=1)

        def proj(lo, post):
            def run():
                val = post(_dot(u, w_ref[:, lo:lo + SLICE_W])).astype(BF16)
                qkv_ref[0, rows, lo:lo + SLICE_W] = val
                stage_ref[wr, :, lo:lo + SLICE_W] = val
            return run

        scale = DK ** -0.5
        post = {0: lambda t: rotary(t * scale), QK: rotary, G_OFF: lambda t: t * scale}
        slices = [proj(lo, post.get(lo, lambda t: t)) for lo in range(0, QKV_W, SLICE_W)]

        def merge_gate(lo):
            def run():
                x = _dot(u, w_ref[:, QKV_W + lo:QKV_W + lo + SLICE_W])
                mg_ref[0, rows, lo:lo + SLICE_W] = jax.nn.sigmoid(x).astype(BF16)
            return run

        m0, m1, m2, m3 = [merge_gate(lo) for lo in range(0, MERGE_W, SLICE_W)]

        def store_g(val):
            g_ref[wr] = val

        low_rank, logit, cumsum = _gla_decay_units(u, w_ref, wup_ref, bup_ref, tri_ref, store_g)

        def sink(branch, h, o):
            off = (0 if branch == "ret" else VW) + h * DV
            of_ref[0, rows, off:off + DV] = o

        units = _scan_units(
            True,
            stage_ref[rd, :, 0:QK], stage_ref[rd, :, QK:V_OFF], stage_ref[rd, :, V_OFF:G_OFF],
            stage_ref[rd, :, G_OFF:G_OFF + QK], stage_ref[rd, :, G_OFF + QK:G_OFF + 2 * QK],
            stage_ref[rd, :, G_OFF + 2 * QK:QKV_W],
            g_ref[rd], _log_gamma_row(rd_ref), qk_ref, mask_ref, state_ref, sink)

        rq_s, rk_s, rv0, rv1, gq_s, gk_s, gv0, gv1 = slices
        fill = [[low_rank, rv0], [logit, rv1, m0], [cumsum, gq_s], [rq_s, m1], [gk_s], [rk_s, m2], [gv0],
                [gv1, m3]]
        _run(_scan_order(fill), units)

    def step(wr0, rd0):
        for half in range(halves):
            half_step(half, wr0 + half, rd0 + half)

    _pipelined(i, step, halves)

    @pl.when(j2 == nb - 1)
    def _():
        sout_ref[0] = state_ref[...]


def _const_spec(shape):
    return pl.BlockSpec(shape, lambda i: (0,) * len(shape), pipeline_mode=pl.Buffered(1))


def _forward_sweep(h, mod, gain, w1, rot, rd_row, wup, bup, s0):
    b, n_tok, _ = h.shape
    halves = _chunks_per_step(n_tok, FWD_CHUNKS_PER_STEP)
    block = halves * CHUNK
    nb = n_tok // block
    n = b * nb
    c1 = lambda i: jnp.minimum(i, n - 1)
    c2 = lambda i: jnp.maximum(i - 1, 0)
    return pl.pallas_call(
        functools.partial(_fwd_kernel, nb=nb, halves=halves),
        grid=(n + 1,),
        in_specs=[
            pl.BlockSpec((1, block, D_MODEL), lambda i: (c1(i), 0, 0)),
            pl.BlockSpec((1, 3, D_MODEL), lambda i: (c1(i) // nb, 0, 0)),
            _const_spec((1, D_MODEL)),
            _const_spec((D_MODEL, FWD_W)),
            pl.BlockSpec((block, DK), lambda i: (c1(i) % nb, 0)),
            pl.BlockSpec((block, DK), lambda i: (c1(i) % nb, 0)),
            pl.BlockSpec((block, DK), lambda i: (c1(i) % nb, 0)),
            _const_spec((1, QK)),
            _const_spec((RANK_PAD, QK)),
            _const_spec((1, QK)),
            _const_spec((CHUNK, 2 * CHUNK)),
            pl.BlockSpec((1, 2 * HEADS, DK, DV), lambda i: (c2(i) // nb, 0, 0, 0)),
        ],
        out_specs=[
            pl.BlockSpec((1, block, QKV_W), lambda i: (c1(i), 0, 0)),
            pl.BlockSpec((1, block, D_MODEL), lambda i: (c1(i), 0, 0)),
            pl.BlockSpec((1, block, MERGE_W), lambda i: (c1(i), 0, 0)),
            pl.BlockSpec((1, block, 2 * VW), lambda i: (c2(i), 0, 0)),
            pl.BlockSpec((1, 2 * HEADS, DK, DV), lambda i: (c2(i) // nb, 0, 0, 0)),
        ],
        out_shape=[
            jax.ShapeDtypeStruct((n, block, QKV_W), BF16),
            jax.ShapeDtypeStruct((n, block, D_MODEL), BF16),
            jax.ShapeDtypeStruct((n, block, MERGE_W), BF16),
            jax.ShapeDtypeStruct((n, block, 2 * VW), F32),
            jax.ShapeDtypeStruct((b, 2 * HEADS, DK, DV), F32),
        ],
        scratch_shapes=[
            pltpu.VMEM((2 * HEADS, DK, DV), F32),
            pltpu.VMEM((2 * halves, CHUNK, QKV_W), BF16),
            pltpu.VMEM((2 * halves, CHUNK, QK), F32),
            pltpu.VMEM((2, CHUNK, QK), BF16),
            pltpu.VMEM((CHUNK, QK), F32),
        ],
        compiler_params=pltpu.CompilerParams(
            dimension_semantics=("arbitrary",), vmem_limit_bytes=VMEM_LIMIT_BYTES),
        name="forward_sweep",
    )(h.reshape(n, block, D_MODEL), mod, gain, w1, *rot, rd_row, wup, bup, _cumsum_matrix(True), s0)


def _bwd_kernel(u_ref, h2_ref, mod2_ref, w_ref, qkv_ref, mg_ref, of_ref, rd_ref, wup_ref, bup_ref,
                tri_ref, rng_ref, gng_ref, wbr_ref, wbg_ref, wo_ref, fng_ref, s0_ref,
                *refs, nb, halves, need_out, final_norm):
    if need_out:
        hout_ref, sout_ref, state_ref, g_ref, qk_ref, mask_ref, gate_ref = refs
    else:
        sout_ref, state_ref, g_ref, qk_ref, mask_ref = refs
    i = pl.program_id(0)
    j2 = lax.rem(jnp.maximum(i - 1, 0), nb)

    @pl.when(i == 0)
    def _():
        _zero_slots(g_ref, halves, halves)
        _store_retention_tables(False, _log_gamma_row(rd_ref), qk_ref, mask_ref)
        if need_out:
            _zero_slots(gate_ref, halves, halves)

    @pl.when(j2 == 0)
    def _():
        state_ref[...] = s0_ref[0]

    def silu(x):
        return x * jax.nn.sigmoid(x)

    def half_step(half, wr, rd):
        rows = slice(half * CHUNK, (half + 1) * CHUNK)
        u = u_ref[0, rows, :]

        def store_g(val):
            g_ref[wr] = val

        low_rank, logit, cumsum = _gla_decay_units(u, w_ref, wup_ref, bup_ref, tri_ref, store_g)

        def gate(lo):
            def run():
                norm_gain = rng_ref if lo < VW else gng_ref
                x = silu(_dot(u, w_ref[:, lo:lo + DV])) * norm_gain[:, lo % VW:lo % VW + DV]
                gate_ref[wr, :, lo:lo + DV] = x.astype(BF16)
            return run

        slices = [gate(lo) for lo in range(0, 2 * VW, DV)] if need_out else []

        acc = {}

        def head_norm(x, center):
            if center:
                x = x - jnp.mean(x, axis=-1, keepdims=True)
            return x * lax.rsqrt(jnp.mean(x * x, axis=-1, keepdims=True) + EPS)

        outs = {}

        def sink(branch, h, o):
            outs[branch, h] = o

        def branch_proj(branch, p):
            def run():
                w_ = wbr_ref if branch == "ret" else wbg_ref
                off = 0 if branch == "ret" else VW
                ys = []
                for h in (2 * p, 2 * p + 1):
                    gs = slice(off + h * DV, off + (h + 1) * DV)
                    x = head_norm(of_ref[0, rows, gs] + outs.pop((branch, h)), branch == "ret")
                    ys.append(x.astype(BF16) * gate_ref[rd, :, gs])
                part = _dot(jnp.concatenate(ys, axis=1), w_[2 * p * DV:(2 * p + 2) * DV, :])
                acc[branch] = part if branch not in acc else acc[branch] + part
            return run

        def out_proj():
            merged = (mg_ref[0, rows, 0:D_MODEL].astype(F32) * acc.pop("ret")
                      + mg_ref[0, rows, D_MODEL:MERGE_W].astype(F32) * acc.pop("gla"))
            acc["out"] = _dot(merged.astype(BF16), wo_ref[...])

        units = _scan_units(
            False,
            qkv_ref[0, rows, 0:QK], qkv_ref[0, rows, QK:V_OFF], qkv_ref[0, rows, V_OFF:G_OFF],
            qkv_ref[0, rows, G_OFF:G_OFF + QK], qkv_ref[0, rows, G_OFF + QK:G_OFF + 2 * QK],
            qkv_ref[0, rows, G_OFF + 2 * QK:QKV_W],
            g_ref[rd], _log_gamma_row(rd_ref), qk_ref, mask_ref, state_ref, sink)

        if not need_out:
            _run(_scan_order([[low_rank], [logit], [cumsum]] + [[]] * 5), units)
            return

        fill = [
            [low_rank, slices[0]],
            [logit, slices[1]],
            [cumsum, slices[2]],
            [slices[3]],
            [branch_proj("ret", 0)],
            [slices[4]],
            [branch_proj("ret", 1), slices[5]],
            [slices[6], branch_proj("gla", 0), slices[7], branch_proj("gla", 1), out_proj],
        ]
        _run(_scan_order(fill), units)

        h_new = h2_ref[0, rows, :] + mod2_ref[0, 2:3, :] * acc.pop("out")
        if final_norm:
            ms = jnp.mean(h_new * h_new, axis=-1, keepdims=True)
            h_new = h_new * lax.rsqrt(ms + EPS) * fng_ref[...]
        hout_ref[0, rows, :] = h_new

    def step(wr0, rd0):
        for half in reversed(range(halves)):
            half_step(half, wr0 + half, rd0 + half)

    _pipelined(i, step, halves)

    @pl.when(j2 == nb - 1)
    def _():
        sout_ref[0] = state_ref[...]


def _backward_sweep(h, u, mod, w2, qkv, merge, o_fwd, rd_row, wup, bup, ret_gain, gla_gain,
                    w_br, w_bg, w_o, final_gain, s0, *, need_out, final_norm):
    b, n_tok, _ = h.shape
    halves = _chunks_per_step(n_tok, BWD_CHUNKS_PER_STEP)
    block = halves * CHUNK
    nb = n_tok // block
    n = b * nb

    def block_of(k):
        return (k // nb) * nb + (nb - 1 - k % nb)

    k1 = lambda i: jnp.minimum(i, n - 1)
    k2 = lambda i: jnp.maximum(i - 1, 0)
    tok1 = pl.BlockSpec((1, block, D_MODEL), lambda i: (block_of(k1(i)), 0, 0))
    tok2 = pl.BlockSpec((1, block, D_MODEL), lambda i: (block_of(k2(i)), 0, 0))
    state_spec = pl.BlockSpec((1, 2 * HEADS, DK, DV), lambda i: (k2(i) // nb, 0, 0, 0))
    out_specs = [state_spec]
    out_shape = [jax.ShapeDtypeStruct((b, 2 * HEADS, DK, DV), F32)]
    scratch = [pltpu.VMEM((2 * HEADS, DK, DV), F32), pltpu.VMEM((2 * halves, CHUNK, QK), F32),
               pltpu.VMEM((2, CHUNK, QK), BF16), pltpu.VMEM((CHUNK, QK), F32)]
    if need_out:
        out_specs = [tok2] + out_specs
        out_shape = [jax.ShapeDtypeStruct((n, block, D_MODEL), F32)] + out_shape
        scratch = scratch + [pltpu.VMEM((2 * halves, CHUNK, 2 * VW), BF16)]
    outs = pl.pallas_call(
        functools.partial(_bwd_kernel, nb=nb, halves=halves, need_out=need_out, final_norm=final_norm),
        grid=(n + 1,),
        in_specs=[
            tok1,
            tok2,
            pl.BlockSpec((1, 3, D_MODEL), lambda i: (k2(i) // nb, 0, 0)),
            _const_spec((D_MODEL, BWD_W)),
            pl.BlockSpec((1, block, QKV_W), lambda i: (block_of(k2(i)), 0, 0)),
            pl.BlockSpec((1, block, MERGE_W), lambda i: (block_of(k2(i)), 0, 0)),
            pl.BlockSpec((1, block, 2 * VW), lambda i: (block_of(k2(i)), 0, 0)),
            _const_spec((1, QK)),
            _const_spec((RANK_PAD, QK)),
            _const_spec((1, QK)),
            _const_spec((CHUNK, 2 * CHUNK)),
            _const_spec((1, VW)),
            _const_spec((1, VW)),
            _const_spec((VW, D_MODEL)),
            _const_spec((VW, D_MODEL)),
            _const_spec((D_MODEL, D_MODEL)),
            _const_spec((1, D_MODEL)),
            state_spec,
        ],
        out_specs=out_specs,
        out_shape=out_shape,
        scratch_shapes=scratch,
        compiler_params=pltpu.CompilerParams(
            dimension_semantics=("arbitrary",), vmem_limit_bytes=VMEM_LIMIT_BYTES),
        name="backward_sweep",
    )(u.reshape(n, block, D_MODEL), h.reshape(n, block, D_MODEL), mod, w2, qkv.reshape(n, block, QKV_W),
      merge.reshape(n, block, MERGE_W),
      o_fwd.reshape(n, block, 2 * VW), rd_row, wup, bup, _cumsum_matrix(False),
      ret_gain, gla_gain, w_br, w_bg, w_o, final_gain, s0)
    if need_out:
        return outs[0].reshape(b, n_tok, D_MODEL), outs[1]
    return None, outs[0]


def _rotary_tables(n_rows):
    r_idx, c_idx = jnp.meshgrid(jnp.arange(n_rows), jnp.arange(GRID_W), indexing="ij")
    r_idx = r_idx.reshape(-1).astype(F32)
    c_idx = c_idx.reshape(-1).astype(F32)
    n_freq = DK // 4
    inv_freq = ROPE_BASE ** (-jnp.arange(n_freq, dtype=F32) / n_freq)
    ang_r = r_idx[:, None] * inv_freq
    ang_c = c_idx[:, None] * inv_freq
    ang = jnp.stack([ang_r, ang_r, ang_c, ang_c], axis=1).reshape(-1, DK)
    cos, sin = jnp.cos(ang), jnp.sin(ang)
    first = (jnp.arange(DK) // n_freq) % 2 == 0
    return cos, jnp.where(first, -sin, 0.0), jnp.where(first, 0.0, sin)


def _split_w_in(w):
    idx = [0]
    for width in (QK, QK, VW, VW, QK, QK, VW, VW, RANK, D_MODEL, D_MODEL):
        idx.append(idx[-1] + width)
    blk = [w[:, idx[i]:idx[i + 1]] for i in range(11)]
    rq, rk, rv, rg, gq, gk, gv, gg, glr, ma, mb = blk
    glr = jnp.pad(glr, ((0, 0), (0, RANK_PAD - RANK)))
    w1 = jnp.concatenate([rq, rk, rv, gq, gk, gv, ma, mb, glr], axis=1).astype(BF16)
    w2 = jnp.concatenate([rg, gg, glr], axis=1).astype(BF16)
    return w1, w2


def kernel(x, c, ctx, c_ctx, norm_gain, w_ada, b_ada, w_in, ret_decay, gla_w_up, gla_b_up, ret_norm_gain,
           gla_norm_gain, w_branch_ret, w_branch_gla, w_out, final_norm_gain):
    batch, n_lat, _ = x.shape
    n_ctx = ctx.shape[1]
    depth = w_in.shape[0]

    cvecs = jnp.concatenate([c, c_ctx[None, :], jnp.zeros((16 - batch - 1, D_MODEL), F32)], axis=0)
    mods = _modulation(cvecs, w_ada, b_ada)

    rot_lat = _rotary_tables(n_lat // GRID_W)
    rot_ctx = (jnp.ones((n_ctx, DK), F32), jnp.zeros((n_ctx, DK), F32), jnp.zeros((n_ctx, DK), F32))
    zero_state = jnp.zeros((batch, 2 * HEADS, DK, DV), F32)
    fgain = final_norm_gain.reshape(1, D_MODEL)

    h_lat, h_ctx = x, ctx
    for l in range(depth):
        last = l == depth - 1
        mod_lat = mods[l, :batch].reshape(batch, 3, D_MODEL)
        mod_ctx = jnp.broadcast_to(mods[l, batch].reshape(1, 3, D_MODEL), (batch, 3, D_MODEL))
        gain = norm_gain[l].reshape(1, D_MODEL)
        w1, w2 = _split_w_in(w_in[l])
        rd_f = jnp.repeat(ret_decay[l, 0], DK).reshape(1, QK)
        rd_b = jnp.repeat(ret_decay[l, 1], DK).reshape(1, QK)
        wup = jnp.pad(gla_w_up[l], ((0, 0), (0, RANK_PAD - RANK), (0, 0))).astype(BF16)
        bup = gla_b_up[l].reshape(2, 1, QK)
        tail = (rd_b, wup[1], bup[1], ret_norm_gain[l].reshape(1, VW), gla_norm_gain[l].reshape(1, VW),
                w_branch_ret[l].astype(BF16), w_branch_gla[l].astype(BF16), w_out[l].astype(BF16), fgain)

        qkv_c, u_c, mg_c, of_c, sf_c = _forward_sweep(h_ctx, mod_ctx, gain, w1, rot_ctx, rd_f, wup[0], bup[0],
                                                      zero_state)
        qkv_l, u_l, mg_l, of_l, _ = _forward_sweep(h_lat, mod_lat, gain, w1, rot_lat, rd_f, wup[0], bup[0], sf_c)
        h_ctx_new, sb_c = _backward_sweep(h_ctx, u_c, mod_ctx, w2, qkv_c, mg_c, of_c, *tail, zero_state,
                                          need_out=not last, final_norm=False)
        h_lat, _ = _backward_sweep(h_lat, u_l, mod_lat, w2, qkv_l, mg_l, of_l, *tail, sb_c,
                                   need_out=True, final_norm=last)
        h_ctx = h_ctx_new
    return h_lat
```
